```python
import math
import jax, jax.numpy as jnp
from jax import lax
import numpy as np

D_MODEL = 2048
BATCH = 4
SEQ = 2048
DEPTH = 1
DEC_BATCH = 32
DEC_SEQ = 1
PAST_LEN = 16384
PAGE_SIZE = 128

HEAD_DIM = 128
H_MOBA = 8
H_FOX = 8
W_A = H_MOBA * HEAD_DIM
W_B = H_FOX * HEAD_DIM
ATTN_SCALE = HEAD_DIM ** -0.5
MOBA_BLOCK = 256
MOBA_TOPK = 3
MOBA_Q_CHUNK = 32
FOX_Q_BLOCK = 128
ROPE_THETA = 10000.0
FORGET_BIAS_LO = 1.0
FORGET_BIAS_HI = 6.0
PEER_HEADS = 8
PEER_NKEYS = 128
PEER_NEXPERTS = PEER_NKEYS * PEER_NKEYS
PEER_DKEY = 256
PEER_TOPK = 16
PEER_CHUNK = 128
PLE_DIM = 256
RMS_EPS = 1e-6
SPLIT_POINTS = (W_A, 2 * W_A, 3 * W_A, 3 * W_A + W_B, 3 * W_A + 2 * W_B, 3 * W_A + 3 * W_B,
                3 * W_A + 3 * W_B + H_FOX, 3 * W_A + 3 * W_B + H_FOX + D_MODEL)
N_IN = 3 * W_A + 3 * W_B + H_FOX + 2 * D_MODEL

kernel_name = 'moba_fox_peer_gated_hybrid_step'


def rmsnorm(x, g):
    xf = x.astype(jnp.float32)
    r = lax.rsqrt(jnp.mean(xf * xf, axis=-1, keepdims=True) + RMS_EPS)
    return (xf * r).astype(x.dtype) * g


def rope(x, pos):
    half = HEAD_DIM // 2
    inv_freq = ROPE_THETA ** (-jnp.arange(half, dtype=jnp.float32) / half)
    ang = pos.astype(jnp.float32)[:, None] * inv_freq[None, :]
    cos = jnp.cos(ang)[None, :, None, :]
    sin = jnp.sin(ang)[None, :, None, :]
    xf = x.astype(jnp.float32)
    x1, x2 = xf[..., :half], xf[..., half:]
    return jnp.concatenate([x1 * cos - x2 * sin, x2 * cos + x1 * sin], axis=-1).astype(x.dtype)


def heads(t, n):
    return t.reshape(t.shape[0], t.shape[1], n, HEAD_DIM)


def project_in(x, pos, g_attn, w_in, b_forget):
    h = rmsnorm(x, g_attn)
    z = h @ w_in
    qa, ka, va, qf, kf, vf, fl, ga, gb = jnp.split(z, SPLIT_POINTS, axis=-1)
    qa = rope(heads(qa, H_MOBA), pos)
    ka = rope(heads(ka, H_MOBA), pos)
    logf = jax.nn.log_sigmoid((fl + b_forget).astype(jnp.float32))
    return qa, ka, heads(va, H_MOBA), heads(qf, H_FOX), heads(kf, H_FOX), heads(vf, H_FOX), logf, ga, gb


def moba_chunk(q, pos, kb, vb, means):
    B, C, H, _ = q.shape
    nb = kb.shape[2]
    q_blk = pos // MOBA_BLOCK
    own = jnp.broadcast_to(q_blk[None, None, :, None], (B, H, C, 1))
    n_sel = min(MOBA_TOPK, nb - 1)
    if n_sel > 0:
        gate = jnp.einsum('bchd,bhnd->bhcn', q, means).astype(jnp.float32)
        fully_past = jnp.arange(nb, dtype=jnp.int32)[None, :] < q_blk[:, None]
        gate = jnp.where(fully_past[None, None], gate, -jnp.inf)
        top_s, top_i = lax.top_k(gate, n_sel)
        sel = jnp.concatenate([top_i.astype(jnp.int32), own], axis=-1)
        sel_ok = jnp.concatenate([jnp.isfinite(top_s), jnp.ones_like(own, dtype=bool)], axis=-1)
    else:
        sel = own
        sel_ok = jnp.ones_like(own, dtype=bool)
    b_idx = jnp.arange(B)[:, None, None, None]
    h_idx = jnp.arange(H)[None, :, None, None]
    k_sel = kb[b_idx, h_idx, sel]
    v_sel = vb[b_idx, h_idx, sel]
    s = jnp.einsum('bchd,bhcsld->bhcsl', q, k_sel).astype(jnp.float32) * ATTN_SCALE
    k_pos = sel[..., None] * MOBA_BLOCK + jnp.arange(MOBA_BLOCK, dtype=jnp.int32)
    ok = sel_ok[..., None] & (k_pos <= pos[None, None, :, None, None])
    s = jnp.where(ok, s, -jnp.inf)
    n_s = sel.shape[-1]
    p = jax.nn.softmax(s.reshape(B, H, C, n_s * MOBA_BLOCK), axis=-1)
    p = p.reshape(B, H, C, n_s, MOBA_BLOCK).astype(v_sel.dtype)
    return jnp.einsum('bhcsl,bhcsld->bchd', p, v_sel)


def moba_attention(q, k, v, q_start):
    B, Tq, H, D = q.shape
    L = k.shape[1]
    nb = -(-L // MOBA_BLOCK)
    pad = nb * MOBA_BLOCK - L

    def blocks(t):
        t = jnp.pad(t, ((0, 0), (0, pad), (0, 0), (0, 0)))
        return t.reshape(B, nb, MOBA_BLOCK, H, D).transpose(0, 3, 1, 2, 4)

    kb, vb = blocks(k), blocks(v)
    means = jnp.mean(kb.astype(jnp.float32), axis=3).astype(q.dtype)
    c = math.gcd(Tq, MOBA_Q_CHUNK)
    nc = Tq // c
    qc = q.reshape(B, nc, c, H, D).transpose(1, 0, 2, 3, 4)
    pc = (q_start + jnp.arange(Tq, dtype=jnp.int32)).reshape(nc, c)
    out = lax.map(lambda a: moba_chunk(a[0], a[1], kb, vb, means), (qc, pc))
    return out.transpose(1, 0, 2, 3, 4).reshape(B, Tq, H, D)


def fox_attention(q, k, v, cum, q_start):
    B, Tq, H, D = q.shape
    L = k.shape[1]
    c = math.gcd(Tq, FOX_Q_BLOCK)
    nc = Tq // c
    qs = q.reshape(B, nc, c, H, D).transpose(1, 0, 2, 3, 4)
    cqs = cum[:, q_start:q_start + Tq].reshape(B, nc, c, H).transpose(1, 0, 2, 3)
    ps = (q_start + jnp.arange(Tq, dtype=jnp.int32)).reshape(nc, c)
    ck = cum.transpose(0, 2, 1)[:, :, None, :]
    k_pos = jnp.arange(L, dtype=jnp.int32)

    def blk(args):
        qb, cqb, pb = args
        s = jnp.einsum('bchd,blhd->bhcl', qb, k).astype(jnp.float32) * ATTN_SCALE
        s = s + (cqb.transpose(0, 2, 1)[..., None] - ck)
        s = jnp.where(k_pos[None, :] <= pb[:, None], s, -jnp.inf)
        p = jax.nn.softmax(s, axis=-1).astype(v.dtype)
        return jnp.einsum('bhcl,blhd->bchd', p, v)

    out = lax.map(blk, (qs, cqs, ps))
    return out.transpose(1, 0, 2, 3, 4).reshape(B, Tq, H, D)


def merge_out(oa, of, ga, gb, w_branch_a, w_branch_b, w_out):
    B, T = oa.shape[:2]
    ya = oa.reshape(B, T, W_A) @ w_branch_a
    yf = of.reshape(B, T, W_B) @ w_branch_b
    return (jax.nn.sigmoid(ga) * ya + jax.nn.sigmoid(gb) * yf) @ w_out


def peer(xn, w_q, sub_k1, sub_k2, u, v):
    T, D = xn.shape
    c = min(PEER_CHUNK, T)
    nc = -(-T // c)
    xc = jnp.pad(xn, ((0, nc * c - T), (0, 0))).reshape(nc, c, D)
    half = PEER_DKEY // 2
    kk = PEER_TOPK * PEER_TOPK

    def chunk(xt):
        q = (xt @ w_q).reshape(c, PEER_HEADS, 2, half)
        s1 = jnp.einsum('chd,nd->chn', q[:, :, 0], sub_k1).astype(jnp.float32)
        s2 = jnp.einsum('chd,nd->chn', q[:, :, 1], sub_k2).astype(jnp.float32)
        t1, i1 = lax.top_k(s1, PEER_TOPK)
        t2, i2 = lax.top_k(s2, PEER_TOPK)
        cand_s = (t1[..., :, None] + t2[..., None, :]).reshape(c, PEER_HEADS, kk)
        cand_e = (i1[..., :, None] * PEER_NKEYS + i2[..., None, :]).reshape(c, PEER_HEADS, kk)
        top_s, top_j = lax.top_k(cand_s, PEER_TOPK)
        e = jnp.take_along_axis(cand_e, top_j, axis=-1)
        g = jax.nn.softmax(top_s, axis=-1)
        a = jax.nn.gelu(jnp.einsum('cd,chkd->chk', xt, u[e]), approximate=False)
        return jnp.einsum('chk,chkd->cd', (g * a).astype(xt.dtype), v[e])

    return lax.map(chunk, xc).reshape(nc * c, D)[:T]


def channel_and_ple(x, p, g_ffn, w_peer_q, sub_k1, sub_k2, u, v, g_ple, w_ple_gate, w_ple_proj):
    B, T, D = x.shape
    x = x + peer(rmsnorm(x, g_ffn).reshape(B * T, D), w_peer_q, sub_k1, sub_k2, u, v).reshape(B, T, D)
    gate = jax.nn.sigmoid(rmsnorm(x, g_ple) @ w_ple_gate)
    return x + gate * (p @ w_ple_proj)


def gather_pages(cache, page_table, layer):
    g = cache[page_table, layer]
    return g.reshape((g.shape[0], g.shape[1] * g.shape[2]) + g.shape[3:])


def setup_inputs(seed: int = 0) -> dict:
    key = jax.random.key(seed)
    ks = jax.random.split(key, 26)
    n_pages = PAST_LEN // PAGE_SIZE
    n_used = DEC_BATCH * n_pages
    n_pool = (5 * n_used + 3) // 4
    f32 = jnp.float32

    def nrm(k, shape, scale):
        return jax.random.normal(k, shape, f32) * scale

    def gain(k, shape):
        return 1.0 + 0.02 * jax.random.normal(k, shape, f32)

    page_table = jax.random.permutation(ks[7], n_pool)[:n_used].reshape(DEC_BATCH, n_pages).astype(jnp.int32)
    return {
        'x_prompt': nrm(ks[0], (BATCH, SEQ, D_MODEL), 1.0),
        'x_sample': nrm(ks[1], (DEC_BATCH, DEC_SEQ, D_MODEL), 1.0),
        'cache_moba_k': nrm(ks[2], (n_pool, DEPTH, PAGE_SIZE, H_MOBA, HEAD_DIM), 1.0),
        'cache_moba_v': nrm(ks[3], (n_pool, DEPTH, PAGE_SIZE, H_MOBA, HEAD_DIM), 1.0),
        'cache_fox_k': nrm(ks[4], (n_pool, DEPTH, PAGE_SIZE, H_FOX, HEAD_DIM), 1.0),
        'cache_fox_v': nrm(ks[5], (n_pool, DEPTH, PAGE_SIZE, H_FOX, HEAD_DIM), 1.0),
        'cache_fox_logf': jax.nn.log_sigmoid(jax.random.uniform(ks[6], (n_pool, DEPTH, PAGE_SIZE, H_FOX), f32,
                                                                FORGET_BIAS_LO, FORGET_BIAS_HI)),
        'page_table': page_table,
        'p_prompt': nrm(ks[8], (DEPTH, BATCH, SEQ, PLE_DIM), 1.0),
        'p_sample': nrm(ks[9], (DEPTH, DEC_BATCH, DEC_SEQ, PLE_DIM), 1.0),
        'g_attn': gain(ks[10], (DEPTH, D_MODEL)),
        'w_in': nrm(ks[11], (DEPTH, D_MODEL, N_IN), D_MODEL ** -0.5),
        'b_forget': jax.random.uniform(ks[12], (DEPTH, H_FOX), f32, FORGET_BIAS_LO, FORGET_BIAS_HI),
        'w_branch_a': nrm(ks[13], (DEPTH, W_A, D_MODEL), W_A ** -0.5),
        'w_branch_b': nrm(ks[14], (DEPTH, W_B, D_MODEL), W_B ** -0.5),
        'w_out': nrm(ks[15], (DEPTH, D_MODEL, D_MODEL), D_MODEL ** -0.5),
        'g_ffn': gain(ks[16], (DEPTH, D_MODEL)),
        'w_peer_q': nrm(ks[17], (DEPTH, D_MODEL, PEER_HEADS * PEER_DKEY), D_MODEL ** -0.5),
        'peer_subkey_1': nrm(ks[18], (DEPTH, PEER_NKEYS, PEER_DKEY // 2), (PEER_DKEY // 2) ** -0.5),
        'peer_subkey_2': nrm(ks[19], (DEPTH, PEER_NKEYS, PEER_DKEY // 2), (PEER_DKEY // 2) ** -0.5),
        'peer_u': nrm(ks[20], (DEPTH, PEER_NEXPERTS, D_MODEL), D_MODEL ** -0.5),
        'peer_v': nrm(ks[21], (DEPTH, PEER_NEXPERTS, D_MODEL), PEER_HEADS ** -0.5),
        'g_ple': gain(ks[22], (DEPTH, D_MODEL)),
        'w_ple_gate': nrm(ks[23], (DEPTH, D_MODEL, D_MODEL), D_MODEL ** -0.5),
        'w_ple_proj': nrm(ks[24], (DEPTH, PLE_DIM, D_MODEL), PLE_DIM ** -0.5),
        'g_final': gain(ks[25], (D_MODEL,)),
    }


def reference(x_prompt, x_sample, cache_moba_k, cache_moba_v, cache_fox_k, cache_fox_v, cache_fox_logf,
              page_table, p_prompt, p_sample, g_attn, w_in, b_forget, w_branch_a, w_branch_b, w_out,
              g_ffn, w_peer_q, peer_subkey_1, peer_subkey_2, peer_u, peer_v, g_ple, w_ple_gate,
              w_ple_proj, g_final):
    pos_p = jnp.arange(SEQ, dtype=jnp.int32)
    pos_s = PAST_LEN + jnp.arange(DEC_SEQ, dtype=jnp.int32)
    xp, xs = x_prompt, x_sample
    mk_p, mv_p, fk_p, fv_p, fl_p = [], [], [], [], []
    mk_s, mv_s, fk_s, fv_s, fl_s = [], [], [], [], []
    for l in range(DEPTH):
        qa, ka, va, qf, kf, vf, lf, ga, gb = project_in(xp, pos_p, g_attn[l], w_in[l], b_forget[l])
        oa = moba_attention(qa, ka, va, 0)
        of = fox_attention(qf, kf, vf, jnp.cumsum(lf, axis=1), 0)
        xp = xp + merge_out(oa, of, ga, gb, w_branch_a[l], w_branch_b[l], w_out[l])
        xp = channel_and_ple(xp, p_prompt[l], g_ffn[l], w_peer_q[l], peer_subkey_1[l], peer_subkey_2[l],
                             peer_u[l], peer_v[l], g_ple[l], w_ple_gate[l], w_ple_proj[l])
        mk_p.append(ka); mv_p.append(va); fk_p.append(kf); fv_p.append(vf); fl_p.append(lf)

        qa, ka, va, qf, kf, vf, lf, ga, gb = project_in(xs, pos_s, g_attn[l], w_in[l], b_forget[l])
        ka_all = jnp.concatenate([gather_pages(cache_moba_k, page_table, l), ka], axis=1)
        va_all = jnp.concatenate([gather_pages(cache_moba_v, page_table, l), va], axis=1)
        oa = moba_attention(qa, ka_all, va_all, PAST_LEN)
        kf_all = jnp.concatenate([gather_pages(cache_fox_k, page_table, l), kf], axis=1)
        vf_all = jnp.concatenate([gather_pages(cache_fox_v, page_table, l), vf], axis=1)
        lf_all = jnp.concatenate([gather_pages(cache_fox_logf, page_table, l).astype(jnp.float32), lf], axis=1)
        of = fox_attention(qf, kf_all, vf_all, jnp.cumsum(lf_all, axis=1), PAST_LEN)
        xs = xs + merge_out(oa, of, ga, gb, w_branch_a[l], w_branch_b[l], w_out[l])
        xs = channel_and_ple(xs, p_sample[l], g_ffn[l], w_peer_q[l], peer_subkey_1[l], peer_subkey_2[l],
                             peer_u[l], peer_v[l], g_ple[l], w_ple_gate[l], w_ple_proj[l])
        mk_s.append(ka); mv_s.append(va); fk_s.append(kf); fv_s.append(vf); fl_s.append(lf)

    y_prompt = rmsnorm(xp, g_final)
    y_sample = rmsnorm(xs, g_final)
    return (y_prompt, y_sample,
            jnp.stack(mk_p, axis=1), jnp.stack(mv_p, axis=1), jnp.stack(fk_p, axis=1),
            jnp.stack(fv_p, axis=1), jnp.stack(fl_p, axis=1),
            jnp.stack(mk_s, axis=1), jnp.stack(mv_s, axis=1), jnp.stack(fk_s, axis=1),
            jnp.stack(fv_s, axis=1), jnp.stack(fl_s, axis=1))
```

```python
import functools
import math

import jax
import jax.numpy as jnp
from jax import lax
from jax.experimental import pallas as pl
from jax.experimental.pallas import tpu as pltpu

F32 = jnp.float32
BF16 = jnp.bfloat16

HEAD_DIM = 128
MOBA_BLOCK = 256
MOBA_TOPK = 3
ROPE_THETA = 10000.0
PEER_TOPK = 16
RMS_EPS = 1e-6
ATTN_SCALE = HEAD_DIM ** -0.5
NEG = -1e30
SQRT_HALF = 0.7071067811865476

V7X_VMEM_LIMIT = 56 * 1024 * 1024
ROW_TILE = 256
PEER_EXPERT_TILE = 1024
DECODE_PAGES_PER_STEP = 4

NT = (((1,), (1,)), ((), ()))
TN = (((0,), (0,)), ((), ()))


def _params(*sem):
    return pltpu.CompilerParams(dimension_semantics=sem, vmem_limit_bytes=V7X_VMEM_LIMIT)


def _resident(shape, index_map):
    return pl.BlockSpec(shape, index_map, pipeline_mode=pl.Buffered(1))


def _rms_scale(x, g):
    r = lax.rsqrt(jnp.mean(x * x, axis=-1, keepdims=True) + RMS_EPS)
    return x * r * g


def _log_sigmoid(x):
    return -(jnp.maximum(-x, 0.0) + jnp.log1p(jnp.exp(-jnp.abs(x))))


def _sigmoid(x):
    return 1.0 / (1.0 + jnp.exp(-x))


def _rms_cast_kernel(x_ref, g_ref, o_ref):
    o_ref[...] = _rms_scale(x_ref[...], g_ref[...]).astype(o_ref.dtype)


def rms_cast(x, g):
    m, d = x.shape
    return pl.pallas_call(
        _rms_cast_kernel,
        out_shape=jax.ShapeDtypeStruct((m, d), BF16),
        grid=(m // ROW_TILE,),
        in_specs=[pl.BlockSpec((ROW_TILE, d), lambda i: (i, 0)), pl.BlockSpec((1, d), lambda i: (0, 0))],
        out_specs=pl.BlockSpec((ROW_TILE, d), lambda i: (i, 0)),
        compiler_params=_params("parallel"),
        name="rms_cast",
    )(x, g.reshape(1, d))


def _proj_qkv_kernel(h_ref, w_ref, cos_ref, sin_ref, of_ref, ob_ref, *, n_rope):
    j = pl.program_id(0)
    z = jnp.dot(h_ref[...], w_ref[...], preferred_element_type=F32)

    @pl.when(j < n_rope)
    def _():
        cos = cos_ref[...]
        sin = sin_ref[...]
        for hh in range(z.shape[1] // HEAD_DIM):
            sl = slice(hh * HEAD_DIM, (hh + 1) * HEAD_DIM)
            zh = z[:, sl]
            r = zh * cos + pltpu.roll(zh, HEAD_DIM // 2, 1) * sin
            of_ref[:, sl] = r
            ob_ref[:, sl] = r.astype(BF16)

    @pl.when(j >= n_rope)
    def _():
        of_ref[...] = z
        ob_ref[...] = z.astype(BF16)


def proj_qkv(h, w, cos, sin, seg, n_rope):
    m, d = h.shape
    n_seg = w.shape[1] // seg
    out_spec = pl.BlockSpec((None, ROW_TILE, seg), lambda j, i: (j, i, 0))
    return pl.pallas_call(
        functools.partial(_proj_qkv_kernel, n_rope=n_rope),
        out_shape=(jax.ShapeDtypeStruct((n_seg, m, seg), F32), jax.ShapeDtypeStruct((n_seg, m, seg), BF16)),
        grid=(n_seg, m // ROW_TILE),
        in_specs=[
            pl.BlockSpec((ROW_TILE, d), lambda j, i: (i, 0)),
            pl.BlockSpec((d, seg), lambda j, i: (0, j)),
            pl.BlockSpec((ROW_TILE, HEAD_DIM), lambda j, i: (i, 0)),
            pl.BlockSpec((ROW_TILE, HEAD_DIM), lambda j, i: (i, 0)),
        ],
        out_specs=(out_spec, out_spec),
        compiler_params=_params("parallel", "parallel"),
        name="proj_qkv",
    )(h, w, cos, sin)


def _proj_sigmoid_kernel(h_ref, w_ref, o_ref):
    o_ref[...] = _sigmoid(jnp.dot(h_ref[...], w_ref[...], preferred_element_type=F32))


def proj_sigmoid(h, w, tn):
    m, d = h.shape
    n = w.shape[1]
    return pl.pallas_call(
        _proj_sigmoid_kernel,
        out_shape=jax.ShapeDtypeStruct((m, n), F32),
        grid=(n // tn, m // ROW_TILE),
        in_specs=[pl.BlockSpec((ROW_TILE, d), lambda j, i: (i, 0)), pl.BlockSpec((d, tn), lambda j, i: (0, j))],
        out_specs=pl.BlockSpec((ROW_TILE, tn), lambda j, i: (i, j)),
        compiler_params=_params("parallel", "parallel"),
        name="proj_gates",
    )(h, w)


def _proj_logf_kernel(h_ref, w_ref, b_ref, o_ref):
    o_ref[...] = _log_sigmoid(jnp.dot(h_ref[...], w_ref[...], preferred_element_type=F32) + b_ref[...])


def proj_logf(h, w, b):
    m, d = h.shape
    n = w.shape[1]
    return pl.pallas_call(
        _proj_logf_kernel,
        out_shape=jax.ShapeDtypeStruct((m, n), F32),
        grid=(m // ROW_TILE,),
        in_specs=[pl.BlockSpec((ROW_TILE, d), lambda i: (i, 0)), pl.BlockSpec((d, n), lambda i: (0, 0)),
                  pl.BlockSpec((1, n), lambda i: (0, 0))],
        out_specs=pl.BlockSpec((ROW_TILE, n), lambda i: (i, 0)),
        compiler_params=_params("parallel"),
        name="proj_logf",
    )(h, w, b)


def _cumsum_kernel(lf_ref, col_ref, row_ref, *, n_heads):
    x = lf_ref[...]
    t = x.shape[0]
    row = lax.broadcasted_iota(jnp.int32, x.shape, 0)
    sh = 1
    while sh < t:
        x = x + jnp.where(row >= sh, pltpu.roll(x, sh, 0), 0.0)
        sh *= 2
    col_ref[...] = x
    row_ref[...] = x.T[:n_heads, :]


def cumsum_logf(lf, batch, seq, n_heads):
    lanes = lf.shape[1]
    return pl.pallas_call(
        functools.partial(_cumsum_kernel, n_heads=n_heads),
        out_shape=(jax.ShapeDtypeStruct((batch * seq, lanes), F32), jax.ShapeDtypeStruct((batch, n_heads, seq), F32)),
        grid=(batch,),
        in_specs=[pl.BlockSpec((seq, lanes), lambda b: (b, 0))],
        out_specs=(pl.BlockSpec((seq, lanes), lambda b: (b, 0)), pl.BlockSpec((None, n_heads, seq), lambda b: (b, 0, 0))),
        compiler_params=_params("parallel"),
        name="cumsum_logf",
    )(lf)


def _softmax_pv(s, v):
    m = jnp.max(s, axis=1, keepdims=True)
    p = jnp.exp(s - m)
    l = jnp.sum(p, axis=1, keepdims=True)
    return jnp.dot(p.astype(BF16), v, preferred_element_type=F32) / l


def _moba_prompt_kernel(q_ref, k_ref, v_ref, kf_ref, o_ref, *, n_blocks):
    qi = pl.program_id(2)
    q = q_ref[...]
    tq = q.shape[0]
    means = jnp.mean(kf_ref[...].reshape(n_blocks, MOBA_BLOCK, HEAD_DIM), axis=1)
    gate = lax.dot_general(q, means.astype(BF16), NT, preferred_element_type=F32)
    n_iota = lax.broadcasted_iota(jnp.int32, gate.shape, 1)
    rank = jnp.zeros(gate.shape, jnp.int32)
    for m in range(n_blocks):
        gm = gate[:, m:m + 1]
        beats = jnp.where(gm > gate, 1, jnp.where(gm == gate, jnp.where(m < n_iota, 1, 0), 0))
        rank = rank + jnp.where(m < qi, beats, 0)
    sel = jnp.where(n_iota < qi, jnp.where(rank < MOBA_TOPK, 1.0, 0.0), 0.0)

    s = lax.dot_general(q, k_ref[...], NT, preferred_element_type=F32) * ATTN_SCALE
    row = lax.broadcasted_iota(jnp.int32, (tq, MOBA_BLOCK), 0)
    col = lax.broadcasted_iota(jnp.int32, (tq, MOBA_BLOCK), 1)
    causal = jnp.where(col <= row, 1.0, 0.0)
    pieces = []
    for n in range(n_blocks):
        ok = jnp.where(qi == n, causal, sel[:, n:n + 1])
        pieces.append(jnp.where(ok > 0.5, s[:, n * MOBA_BLOCK:(n + 1) * MOBA_BLOCK], NEG))
    o_ref[...] = _softmax_pv(jnp.concatenate(pieces, axis=1), v_ref[...]).astype(o_ref.dtype)


def moba_prompt(zb, zf, batch, seq, n_heads):
    m, w = zb.shape[1:]
    nq = seq // MOBA_BLOCK
    kv_map = lambda seg: (lambda b, h, qi: (seg, b, h))
    return pl.pallas_call(
        functools.partial(_moba_prompt_kernel, n_blocks=nq),
        out_shape=jax.ShapeDtypeStruct((m, w), BF16),
        grid=(batch, n_heads, nq),
        in_specs=[
            pl.BlockSpec((None, MOBA_BLOCK, HEAD_DIM), lambda b, h, qi: (0, b * nq + qi, h)),
            pl.BlockSpec((None, seq, HEAD_DIM), kv_map(1)),
            pl.BlockSpec((None, seq, HEAD_DIM), kv_map(2)),
            pl.BlockSpec((None, seq, HEAD_DIM), kv_map(1)),
        ],
        out_specs=pl.BlockSpec((MOBA_BLOCK, HEAD_DIM), lambda b, h, qi: (b * nq + qi, h)),
        compiler_params=_params("parallel", "parallel", "parallel"),
        name="moba_prompt",
    )(zb, zb, zb, zf)


def _fox_prompt_kernel(q_ref, k_ref, v_ref, ccol_ref, crow_ref, o_ref):
    h = pl.program_id(1)
    qi = pl.program_id(2)
    q = q_ref[...]
    tq = q.shape[0]
    s = lax.dot_general(q, k_ref[...], NT, preferred_element_type=F32) * ATTN_SCALE
    ccol = ccol_ref[...]
    lane = lax.broadcasted_iota(jnp.int32, ccol.shape, 1)
    cq = jnp.sum(jnp.where(lane == h, ccol, 0.0), axis=1, keepdims=True)
    ck = crow_ref[pl.ds(h, 1), :]
    s = s + (cq - ck)
    row = lax.broadcasted_iota(jnp.int32, s.shape, 0) + qi * tq
    col = lax.broadcasted_iota(jnp.int32, s.shape, 1)
    s = jnp.where(col <= row, s, NEG)
    o_ref[...] = _softmax_pv(s, v_ref[...]).astype(o_ref.dtype)


def fox_prompt(zb, cum_col, cum_row, batch, seq, n_heads, tq):
    m, w = zb.shape[1:]
    nq = seq // tq
    kv_map = lambda seg: (lambda b, h, qi: (seg, b, h))
    return pl.pallas_call(
        _fox_prompt_kernel,
        out_shape=jax.ShapeDtypeStruct((m, w), BF16),
        grid=(batch, n_heads, nq),
        in_specs=[
            pl.BlockSpec((None, tq, HEAD_DIM), lambda b, h, qi: (3, b * nq + qi, h)),
            pl.BlockSpec((None, seq, HEAD_DIM), kv_map(4)),
            pl.BlockSpec((None, seq, HEAD_DIM), kv_map(5)),
            pl.BlockSpec((tq, cum_col.shape[1]), lambda b, h, qi: (b * nq + qi, 0)),
            pl.BlockSpec((None, n_heads, seq), lambda b, h, qi: (b, 0, 0)),
        ],
        out_specs=pl.BlockSpec((tq, HEAD_DIM), lambda b, h, qi: (b * nq + qi, h)),
        compiler_params=_params("parallel", "parallel", "parallel"),
        name="fox_prompt",
    )(zb, zb, zb, cum_col, cum_row)


def _block_diag_rows(vec, n_heads):
    w = vec.shape[1]
    row = lax.broadcasted_iota(jnp.int32, (n_heads, w), 0)
    col = lax.broadcasted_iota(jnp.int32, (n_heads, w), 1)
    diag = (col // HEAD_DIM) == row
    return jnp.where(diag, jnp.broadcast_to(vec.astype(F32), (n_heads, w)), 0.0), diag


def _moba_gate_kernel(pt_ref, q_ref, *refs, pages, n_blocks, n_heads):
    del pt_ref
    k_refs = refs[:pages]
    o_ref = refs[pages]
    means_ref = refs[pages + 1]
    j = pl.program_id(1)
    per_block = MOBA_BLOCK // k_refs[0].shape[0]
    blocks_per_step = pages // per_block
    for r in range(blocks_per_step):
        tot = jnp.sum(k_refs[r * per_block][...], axis=0, keepdims=True)
        for t in range(1, per_block):
            tot = tot + jnp.sum(k_refs[r * per_block + t][...], axis=0, keepdims=True)
        means_ref[pl.ds(j * blocks_per_step + r, 1), :] = tot * (1.0 / MOBA_BLOCK)

    @pl.when(j == pl.num_programs(1) - 1)
    def _():
        qbd, _ = _block_diag_rows(q_ref[...], n_heads)
        gate = lax.dot_general(qbd.astype(BF16), means_ref[...].astype(BF16), NT,
                               preferred_element_type=F32)
        n_iota = lax.broadcasted_iota(jnp.int32, gate.shape, 1).astype(F32)
        lane = lax.broadcasted_iota(jnp.int32, o_ref.shape, 1)
        out = jnp.full(o_ref.shape, -1.0, F32)
        for r in range(MOBA_TOPK):
            mx = jnp.max(gate, axis=1, keepdims=True)
            ix = jnp.min(jnp.where(gate == mx, n_iota, float(n_blocks)), axis=1, keepdims=True)
            keep = jnp.where(jnp.abs(mx) < jnp.inf, ix, -1.0)
            out = jnp.where(lane == r, keep, out)
            gate = jnp.where(n_iota == ix, -jnp.inf, gate)
        o_ref[...] = out.astype(jnp.int32)


def moba_decode_select(page_table, q_b, cache_k, layer, depth, n_heads):
    bsz, n_pages = page_table.shape
    page, w = cache_k.shape[1:]
    pages = DECODE_PAGES_PER_STEP
    n_blocks = n_pages * page // MOBA_BLOCK

    def k_map(r):
        return lambda b, j, pt: (pt[b, j * pages + r] * depth + layer, 0, 0)

    grid_spec = pltpu.PrefetchScalarGridSpec(
        num_scalar_prefetch=1,
        grid=(bsz, n_pages // pages),
        in_specs=[pl.BlockSpec((None, 1, w), lambda b, j, pt: (b, 0, 0))]
        + [pl.BlockSpec((None, page, w), k_map(r)) for r in range(pages)],
        out_specs=pl.BlockSpec((None, n_heads, 128), lambda b, j, pt: (b, 0, 0)),
        scratch_shapes=[pltpu.VMEM((n_blocks, w), F32)],
    )
    return pl.pallas_call(
        functools.partial(_moba_gate_kernel, pages=pages, n_blocks=n_blocks, n_heads=n_heads),
        out_shape=jax.ShapeDtypeStruct((bsz, n_heads, 128), jnp.int32),
        grid_spec=grid_spec,
        compiler_params=_params("parallel", "arbitrary"),
        name="moba_decode_select",
    )(page_table, q_b.reshape(bsz, 1, w), *([cache_k] * pages))


def _moba_decode_attn_kernel(pt_ref, sel_ref, q_ref, kn_ref, vn_ref, *refs, n_sel, per_block, n_heads):
    del pt_ref
    n = n_sel * per_block
    k_refs = refs[:n]
    v_refs = refs[n:2 * n]
    o_ref = refs[2 * n]
    b = pl.program_id(0)
    h = pl.program_id(1)
    q = q_ref[...]
    q8 = jnp.broadcast_to(q, (8, HEAD_DIM))
    s_self = jnp.sum(q.astype(F32) * kn_ref[...].astype(F32), axis=1, keepdims=True) * ATTN_SCALE
    scores = []
    for r in range(n_sel):
        valid = sel_ref[(b * n_heads + h) * n_sel + r] >= 0
        for t in range(per_block):
            kb = k_refs[r * per_block + t][...].astype(BF16)
            s = lax.dot_general(q8, kb, NT, preferred_element_type=F32) * ATTN_SCALE
            scores.append(jnp.where(valid, s, NEG))
    s_all = jnp.concatenate(scores, axis=1)
    m = jnp.maximum(jnp.max(s_all, axis=1, keepdims=True), s_self)
    p = jnp.exp(s_all - m)
    p_self = jnp.exp(s_self - m)
    l = jnp.sum(p, axis=1, keepdims=True) + p_self
    v_all = jnp.concatenate([v_refs[i][...].astype(BF16) for i in range(n)], axis=0)
    acc = jnp.dot(p.astype(BF16), v_all, preferred_element_type=F32)
    acc = acc + p_self.astype(BF16).astype(F32) * vn_ref[...].astype(F32)
    o_ref[...] = (acc / l)[0:1, :].astype(o_ref.dtype)


def moba_decode_attend(page_table, sel, q_b, k_new_b, v_new_b, cache_k, cache_v, layer, depth, n_heads):
    bsz, n_pages = page_table.shape
    page = cache_k.shape[1]
    per_block = MOBA_BLOCK // page
    n_sel = sel.shape[-1]

    def kv_map(r, t):
        def index(b, h, pt, sl):
            blk = jnp.maximum(sl[(b * n_heads + h) * n_sel + r], 0)
            return (pt[b * n_pages + blk * per_block + t] * depth + layer, 0, h)
        return index

    head_vec = pl.BlockSpec((None, 1, HEAD_DIM), lambda b, h, pt, sl: (b * n_heads + h, 0, 0))
    kv_specs = [pl.BlockSpec((None, page, HEAD_DIM), kv_map(r, t)) for r in range(n_sel) for t in range(per_block)]
    grid_spec = pltpu.PrefetchScalarGridSpec(
        num_scalar_prefetch=2,
        grid=(bsz, n_heads),
        in_specs=[head_vec, head_vec, head_vec] + kv_specs + kv_specs,
        out_specs=head_vec,
    )
    per_head = lambda a: a.reshape(bsz * n_heads, 1, HEAD_DIM)
    n = n_sel * per_block
    out = pl.pallas_call(
        functools.partial(_moba_decode_attn_kernel, n_sel=n_sel, per_block=per_block, n_heads=n_heads),
        out_shape=jax.ShapeDtypeStruct((bsz * n_heads, 1, HEAD_DIM), BF16),
        grid_spec=grid_spec,
        compiler_params=_params("parallel", "parallel"),
        name="moba_decode_attend",
    )(page_table.reshape(-1), sel.reshape(-1), per_head(q_b), per_head(k_new_b), per_head(v_new_b),
      *([cache_k] * n), *([cache_v] * n))
    return out.reshape(bsz, n_heads * HEAD_DIM)


def _fox_decode_kernel(pt_ref, q_ref, kn_ref, vn_ref, lfn_ref, *refs, pages, n_heads):
    del pt_ref
    k_refs = refs[:pages]
    v_refs = refs[pages:2 * pages]
    lf_refs = refs[2 * pages:3 * pages]
    o_ref = refs[3 * pages]
    m_ref, l_ref, carry_ref, acc_ref = refs[3 * pages + 1:]
    j = pl.program_id(1)
    qbd_f, diag = _block_diag_rows(q_ref[...], n_heads)
    qbd = qbd_f.astype(BF16)
    page = k_refs[0].shape[0]
    lanes = lax.broadcasted_iota(jnp.int32, (n_heads, page), 1)

    @pl.when(j == 0)
    def _():
        s_self = jnp.sum(qbd_f * kn_ref[...].astype(F32), axis=1, keepdims=True) * ATTN_SCALE
        m_ref[...] = jnp.broadcast_to(s_self, m_ref.shape)
        l_ref[...] = jnp.ones(l_ref.shape, F32)
        acc_ref[...] = jnp.broadcast_to(vn_ref[...].astype(F32), acc_ref.shape)
        row = lax.broadcasted_iota(jnp.int32, (n_heads, lfn_ref.shape[1]), 0)
        col = lax.broadcasted_iota(jnp.int32, (n_heads, lfn_ref.shape[1]), 1)
        lf_new = jnp.sum(jnp.where(row == col, jnp.broadcast_to(lfn_ref[...], row.shape), 0.0), axis=1, keepdims=True)
        carry_ref[...] = jnp.broadcast_to(lf_new, carry_ref.shape)

    for r in range(pages):
        lf = lf_refs[r][...]
        suf = lf
        sh = 1
        while sh < page:
            suf = suf + jnp.where(lanes + sh < page, pltpu.roll(suf, page - sh, 1), 0.0)
            sh *= 2
        carry = carry_ref[...]
        bias = carry + (suf - lf)
        carry_ref[...] = carry + jnp.broadcast_to(suf[:, 0:1], carry.shape)
        kb = k_refs[r][...].astype(BF16)
        s = lax.dot_general(qbd, kb, NT, preferred_element_type=F32) * ATTN_SCALE + bias
        m_old = m_ref[...]
        m_new = jnp.maximum(m_old, jnp.max(s, axis=1, keepdims=True))
        alpha = jnp.exp(m_old - m_new)
        p = jnp.exp(s - m_new)
        l_ref[...] = alpha * l_ref[...] + jnp.sum(p, axis=1, keepdims=True)
        m_ref[...] = m_new
        pv = jnp.dot(p.astype(BF16), v_refs[r][...].astype(BF16), preferred_element_type=F32)
        acc_ref[...] = alpha[:, 0:1] * acc_ref[...] + pv

    @pl.when(j == pl.num_programs(1) - 1)
    def _():
        out = acc_ref[...] / l_ref[:, 0:1]
        o_ref[...] = jnp.sum(jnp.where(diag, out, 0.0), axis=0, keepdims=True).astype(o_ref.dtype)


def fox_decode(page_table, q_b, k_new_b, v_new_b, lf_new, cache_k, cache_v, cache_lf_t, layer, depth, n_heads):
    bsz, n_pages = page_table.shape
    page, w = cache_k.shape[1:]
    pages = DECODE_PAGES_PER_STEP

    def page_map(r):
        return lambda b, j, pt: (pt[b, n_pages - 1 - (j * pages + r)] * depth + layer, 0, 0)

    row_vec = pl.BlockSpec((None, 1, w), lambda b, j, pt: (b, 0, 0))
    grid_spec = pltpu.PrefetchScalarGridSpec(
        num_scalar_prefetch=1,
        grid=(bsz, n_pages // pages),
        in_specs=[row_vec, row_vec, row_vec, pl.BlockSpec((None, 1, lf_new.shape[1]), lambda b, j, pt: (b, 0, 0))]
        + [pl.BlockSpec((None, page, w), page_map(r)) for r in range(pages)]
        + [pl.BlockSpec((None, page, w), page_map(r)) for r in range(pages)]
        + [pl.BlockSpec((None, n_heads, page), page_map(r)) for r in range(pages)],
        out_specs=row_vec,
        scratch_shapes=[pltpu.VMEM((n_heads, page), F32), pltpu.VMEM((n_heads, page), F32),
                        pltpu.VMEM((n_heads, page), F32), pltpu.VMEM((n_heads, w), F32)],
    )
    rows = lambda a: a.reshape(bsz, 1, a.shape[-1])
    out = pl.pallas_call(
        functools.partial(_fox_decode_kernel, pages=pages, n_heads=n_heads),
        out_shape=jax.ShapeDtypeStruct((bsz, 1, w), BF16),
        grid_spec=grid_spec,
        compiler_params=_params("parallel", "arbitrary"),
        name="fox_decode",
    )(page_table, rows(q_b), rows(k_new_b), rows(v_new_b), rows(lf_new),
      *([cache_k] * pages), *([cache_v] * pages), *([cache_lf_t] * pages))
    return out.reshape(bsz, w)


def _merge_kernel(oa_ref, of_ref, ga_ref, gb_ref, wa_ref, wb_ref, wo_ref, x_ref, g_ref, x1_ref, xn_ref):
    ya = jnp.dot(oa_ref[...], wa_ref[...], preferred_element_type=F32)
    yf = jnp.dot(of_ref[...], wb_ref[...], preferred_element_type=F32)
    mix = ga_ref[...] * ya + gb_ref[...] * yf
    x1 = x_ref[...] + jnp.dot(mix.astype(BF16), wo_ref[...], preferred_element_type=F32)
    x1_ref[...] = x1
    xn_ref[...] = _rms_scale(x1, g_ref[...]).astype(BF16)


def merge_out(oa, of, gates, wa, wb, wo, x, g_next):
    m, d = x.shape
    wdt = oa.shape[1]
    row = lambda width: pl.BlockSpec((ROW_TILE, width), lambda i: (i, 0))
    return pl.pallas_call(
        _merge_kernel,
        out_shape=(jax.ShapeDtypeStruct((m, d), F32), jax.ShapeDtypeStruct((m, d), BF16)),
        grid=(m // ROW_TILE,),
        in_specs=[row(wdt), row(wdt), row(d), pl.BlockSpec((ROW_TILE, d), lambda i: (i, 1)),
                  _resident(wa.shape, lambda i: (0, 0)), _resident(wb.shape, lambda i: (0, 0)),
                  _resident(wo.shape, lambda i: (0, 0)), row(d), _resident((1, d), lambda i: (0, 0))],
        out_specs=(row(d), row(d)),
        compiler_params=_params("parallel"),
        name="merge_out",
    )(oa, of, gates, gates, wa, wb, wo, x, g_next.reshape(1, d))


def _matmul_bf16_kernel(a_ref, w_ref, o_ref):
    o_ref[...] = jnp.dot(a_ref[...], w_ref[...], preferred_element_type=F32).astype(o_ref.dtype)


def matmul_bf16(a, w, tn):
    m, d = a.shape
    n = w.shape[1]
    return pl.pallas_call(
        _matmul_bf16_kernel,
        out_shape=jax.ShapeDtypeStruct((m, n), BF16),
        grid=(n // tn, m // ROW_TILE),
        in_specs=[pl.BlockSpec((ROW_TILE, d), lambda j, i: (i, 0)), pl.BlockSpec((d, tn), lambda j, i: (0, j))],
        out_specs=pl.BlockSpec((ROW_TILE, tn), lambda j, i: (i, j)),
        compiler_params=_params("parallel", "parallel"),
        name="peer_query",
    )(a, w)


def _top_values(x, k):
    vals = []
    for _ in range(k):
        mx = jnp.max(x, axis=0, keepdims=True)
        vals.append(mx)
        x = jnp.where(x == mx, NEG, x)
    return vals


def _peer_route_kernel(q_ref, k1_ref, k2_ref, s1_ref, s2_ref, e1_ref, e2_ref, thr_ref, *, n_heads):
    dk = k1_ref.shape[1]
    k1 = k1_ref[...]
    k2 = k2_ref[...]
    for h in range(n_heads):
        q1 = q_ref[:, (2 * h) * dk:(2 * h + 1) * dk]
        q2 = q_ref[:, (2 * h + 1) * dk:(2 * h + 2) * dk]
        s1 = lax.dot_general(k1, q1, NT, preferred_element_type=F32)
        s2 = lax.dot_general(k2, q2, NT, preferred_element_type=F32)
        t1 = _top_values(s1, PEER_TOPK)
        t2 = _top_values(s2, PEER_TOPK)
        t2_all = jnp.concatenate(t2, axis=0)
        cand = jnp.concatenate([t1[a] + t2_all for a in range(PEER_TOPK)], axis=0)
        top = _top_values(cand, PEER_TOPK)
        z = jnp.zeros_like(top[0])
        for val in top:
            z = z + jnp.exp(val - top[0])
        s1_ref[h] = s1
        s2_ref[h] = s2
        e1_ref[h] = jnp.exp(s1 - t1[0])
        e2_ref[h] = jnp.exp(s2 - t2[0]) / z
        thr_ref[pl.ds(h, 1), :] = top[PEER_TOPK - 1]


def peer_route(qp, k1, k2, n_heads):
    m, w = qp.shape
    n_keys = k1.shape[0]
    big = jax.ShapeDtypeStruct((n_heads, n_keys, m), F32)
    big_spec = pl.BlockSpec((n_heads, n_keys, ROW_TILE), lambda i: (0, 0, i))
    return pl.pallas_call(
        functools.partial(_peer_route_kernel, n_heads=n_heads),
        out_shape=(big, big, big, big, jax.ShapeDtypeStruct((n_heads, m), F32)),
        grid=(m // ROW_TILE,),
        in_specs=[pl.BlockSpec((ROW_TILE, w), lambda i: (i, 0)), pl.BlockSpec(k1.shape, lambda i: (0, 0)),
                  pl.BlockSpec(k2.shape, lambda i: (0, 0))],
        out_specs=(big_spec, big_spec, big_spec, big_spec, pl.BlockSpec((n_heads, ROW_TILE), lambda i: (0, i))),
        compiler_params=_params("parallel"),
        name="peer_route",
    )(qp, k1, k2)


def _peer_dense_kernel(xn_ref, u_ref, v_ref, s1_ref, s2_ref, e1_ref, e2_ref, thr_ref, x_ref, o_ref,
                       at_ref, cg_ref, *, n_heads, n_keys):
    e = pl.program_id(1)
    te, tm = at_ref.shape
    groups = te // n_keys
    chunk = 16
    chunks = n_keys // chunk

    @pl.when(e == 0)
    def _():
        o_ref[...] = x_ref[...]

    at_ref[...] = lax.dot_general(u_ref[...], xn_ref[...], NT, preferred_element_type=F32)

    def body(it, carry):
        ig = it // chunks
        c = it % chunks
        i = e * groups + ig
        r0 = pl.multiple_of(c * chunk, chunk)
        acc = jnp.zeros((chunk, tm), F32)
        for h in range(n_heads):
            s1_row = s1_ref[h, pl.ds(i, 1), :]
            e1_row = e1_ref[h, pl.ds(i, 1), :]
            pair = s2_ref[h, pl.ds(r0, chunk), :] + s1_row
            wgt = e2_ref[h, pl.ds(r0, chunk), :] * e1_row
            acc = jnp.where(pair >= thr_ref[pl.ds(h, 1), :], acc + wgt, acc)
        a0 = pl.multiple_of(ig * n_keys + c * chunk, chunk)
        a = at_ref[pl.ds(a0, chunk), :]
        gelu = 0.5 * a * (1.0 + lax.erf(a * SQRT_HALF))
        cg_ref[pl.ds(a0, chunk), :] = (acc * gelu).astype(BF16)
        return carry

    lax.fori_loop(0, groups * chunks, body, 0)
    o_ref[...] += lax.dot_general(cg_ref[...], v_ref[...], TN, preferred_element_type=F32)


def peer_dense(xn, u, v, s1, s2, e1, e2, thr, x):
    m, d = xn.shape
    n_exp = u.shape[0]
    n_heads, n_keys = s1.shape[:2]
    te = PEER_EXPERT_TILE
    big_spec = pl.BlockSpec((n_heads, n_keys, ROW_TILE), lambda i, e: (0, 0, i))
    return pl.pallas_call(
        functools.partial(_peer_dense_kernel, n_heads=n_heads, n_keys=n_keys),
        out_shape=jax.ShapeDtypeStruct((m, d), F32),
        grid=(m // ROW_TILE, n_exp // te),
        in_specs=[pl.BlockSpec((ROW_TILE, d), lambda i, e: (i, 0)),
                  pl.BlockSpec((te, d), lambda i, e: (e, 0)),
                  pl.BlockSpec((te, d), lambda i, e: (e, 0)),
                  big_spec, big_spec, big_spec, big_spec,
                  pl.BlockSpec((n_heads, ROW_TILE), lambda i, e: (0, i)),
                  pl.BlockSpec((ROW_TILE, d), lambda i, e: (i, 0))],
        out_specs=pl.BlockSpec((ROW_TILE, d), lambda i, e: (i, 0)),
        scratch_shapes=[pltpu.VMEM((te, ROW_TILE), F32), pltpu.VMEM((te, ROW_TILE), BF16)],
        compiler_params=_params("parallel", "arbitrary"),
        name="peer_dense",
    )(xn, u, v, s1, s2, e1, e2, thr, x)


def _ple_kernel(x_ref, p_ref, g_ref, wg_ref, wp_ref, gf_ref, y_ref):
    x = x_ref[...]
    gate = _sigmoid(jnp.dot(_rms_scale(x, g_ref[...]).astype(BF16), wg_ref[...], preferred_element_type=F32))
    emb = jnp.dot(p_ref[...].astype(BF16), wp_ref[...], preferred_element_type=F32)
    y_ref[...] = _rms_scale(x + gate * emb, gf_ref[...])


def ple_final(x, p, g_ple, wg, wp, g_final):
    m, d = x.shape
    row = lambda width: pl.BlockSpec((ROW_TILE, width), lambda i: (i, 0))
    vec = _resident((1, d), lambda i: (0, 0))
    return pl.pallas_call(
        _ple_kernel,
        out_shape=jax.ShapeDtypeStruct((m, d), F32),
        grid=(m // ROW_TILE,),
        in_specs=[row(d), row(p.shape[1]), vec, _resident(wg.shape, lambda i: (0, 0)),
                  _resident(wp.shape, lambda i: (0, 0)), vec],
        out_specs=row(d),
        compiler_params=_params("parallel"),
        name="ple_final",
    )(x, p, g_ple.reshape(1, d), wg, wp, g_final.reshape(1, d))


def _rope_tables(pos):
    half = HEAD_DIM // 2
    inv_freq = ROPE_THETA ** (-jnp.arange(half, dtype=F32) / half)
    ang = pos.astype(F32)[:, None] * inv_freq[None, :]
    cos, sin = jnp.cos(ang), jnp.sin(ang)
    return jnp.concatenate([cos, cos], axis=1), jnp.concatenate([-sin, sin], axis=1)


def _pad_rows(a, rows):
    return jnp.pad(a, ((0, rows - a.shape[0]),) + ((0, 0),) * (a.ndim - 1))


def _token_stage_in(x, pos, g_attn, w_qkv, w_fl, b_fl, w_gates, seg):
    h = rms_cast(x, g_attn)
    cos, sin = _rope_tables(pos)
    zf, zb = proj_qkv(h, w_qkv, cos, sin, seg, n_rope=2)
    gates = proj_sigmoid(h, w_gates, tn=1024)
    lf = proj_logf(h, w_fl, b_fl)
    return zf, zb, gates, lf


def _token_stage_out(x, oa, of, gates, p, wts):
    x1, xn = merge_out(oa, of, gates, wts["wa"], wts["wb"], wts["wo"], x, wts["g_ffn"])
    qp = matmul_bf16(xn, wts["wq"], tn=1024)
    s1, s2, e1, e2, thr = peer_route(qp, wts["k1"], wts["k2"], wts["peer_heads"])
    x2 = peer_dense(xn, wts["u"], wts["v"], s1, s2, e1, e2, thr, x1)
    return ple_final(x2, p, wts["g_ple"], wts["wg"], wts["wp"], wts["g_final"])


def kernel(x_prompt, x_sample, cache_moba_k, cache_moba_v, cache_fox_k, cache_fox_v, cache_fox_logf, page_table, p_prompt, p_sample, g_attn, w_in, b_forget, w_branch_a, w_branch_b, w_out, g_ffn, w_peer_q, peer_subkey_1, peer_subkey_2, peer_u, peer_v, g_ple, w_ple_gate, w_ple_proj, g_final):
    batch, seq, d = x_prompt.shape
    dec_batch, dec_seq, _ = x_sample.shape
    n_pool, depth, page, n_moba, hd = cache_moba_k.shape
    n_fox = cache_fox_k.shape[3]
    n_pages = page_table.shape[1]
    past_len = n_pages * page
    wa_w, wb_w = n_moba * hd, n_fox * hd
    assert depth == 1 and dec_seq == 1 and hd == HEAD_DIM and wa_w == wb_w
    assert seq % MOBA_BLOCK == 0 and past_len % MOBA_BLOCK == 0 and MOBA_BLOCK % page == 0
    layer = 0
    n_qkv = 3 * wa_w + 3 * wb_w
    peer_heads = w_peer_q.shape[2] // (2 * peer_subkey_1.shape[2])
    lanes = 128

    w_l = w_in[layer]
    w_qkv = w_l[:, :n_qkv].astype(BF16)
    w_fl = jnp.pad(w_l[:, n_qkv:n_qkv + n_fox], ((0, 0), (0, lanes - n_fox))).astype(BF16)
    b_fl = jnp.pad(b_forget[layer], (0, lanes - n_fox)).reshape(1, lanes)
    w_gates = w_l[:, n_qkv + n_fox:].astype(BF16)
    wts = dict(
        wa=w_branch_a[layer].astype(BF16), wb=w_branch_b[layer].astype(BF16), wo=w_out[layer].astype(BF16),
        g_ffn=g_ffn[layer], wq=w_peer_q[layer].astype(BF16),
        k1=peer_subkey_1[layer].astype(BF16), k2=peer_subkey_2[layer].astype(BF16), peer_heads=peer_heads,
        u=peer_u[layer].astype(BF16), v=peer_v[layer].astype(BF16),
        g_ple=g_ple[layer], wg=w_ple_gate[layer].astype(BF16), wp=w_ple_proj[layer].astype(BF16), g_final=g_final,
    )

    m_p = batch * seq
    xp = x_prompt.reshape(m_p, d)
    pos_p = jnp.tile(jnp.arange(seq, dtype=jnp.int32), batch)
    zf_p, zb_p, gates_p, lf_p = _token_stage_in(xp, pos_p, g_attn[layer].reshape(1, d), w_qkv, w_fl, b_fl, w_gates, wa_w)
    cum_col, cum_row = cumsum_logf(lf_p, batch, seq, n_fox)
    oa_p = moba_prompt(zb_p, zf_p, batch, seq, n_moba)
    of_p = fox_prompt(zb_p, cum_col, cum_row, batch, seq, n_fox, tq=MOBA_BLOCK)
    y_p = _token_stage_out(xp, oa_p, of_p, gates_p, p_prompt[layer].reshape(m_p, -1), wts)

    m_s = ROW_TILE * (-(-dec_batch // ROW_TILE))
    xs = _pad_rows(x_sample.reshape(dec_batch, d), m_s)
    pos_s = jnp.full((m_s,), past_len, jnp.int32)
    zf_s, zb_s, gates_s, lf_s = _token_stage_in(xs, pos_s, g_attn[layer].reshape(1, d), w_qkv, w_fl, b_fl, w_gates, wa_w)
    zb_d = zb_s[:, :dec_batch]
    flat = lambda c: c.reshape(n_pool * depth, page, -1)
    sel = moba_decode_select(page_table, zb_d[0], flat(cache_moba_k), layer, depth, n_moba)[:, :, :MOBA_TOPK]
    oa_s = moba_decode_attend(page_table, sel, zb_d[0], zb_d[1], zb_d[2], flat(cache_moba_k), flat(cache_moba_v),
                              layer, depth, n_moba)
    lf_t = jnp.swapaxes(cache_fox_logf, 2, 3).reshape(n_pool * depth, n_fox, page)
    of_s = fox_decode(page_table, zb_d[3], zb_d[4], zb_d[5], lf_s[:dec_batch], flat(cache_fox_k), flat(cache_fox_v),
                      lf_t, layer, depth, n_fox)
    y_s = _token_stage_out(xs, _pad_rows(oa_s, m_s), _pad_rows(of_s, m_s), gates_s,
                           _pad_rows(p_sample[layer].reshape(dec_batch, -1), m_s), wts)

    def kv_p(a, n):
        return a.reshape(batch, 1, seq, n, hd)

    def kv_s(a, n):
        return a[:dec_batch].reshape(dec_batch, 1, 1, n, hd)

    return (y_p.reshape(batch, seq, d), y_s[:dec_batch].reshape(dec_batch, 1, d),
            kv_p(zf_p[1], n_moba), kv_p(zf_p[2], n_moba), kv_p(zf_p[4], n_fox), kv_p(zf_p[5], n_fox),
            lf_p[:, :n_fox].reshape(batch, 1, seq, n_fox),
            kv_s(zf_s[1], n_moba), kv_s(zf_s[2], n_moba), kv_s(zf_s[4], n_fox), kv_s(zf_s[5], n_fox),
            lf_s[:dec_batch, :n_fox].reshape(dec_batch, 1, 1, n_fox))
```

```python
import functools
import math

import jax
import jax.numpy as jnp
from jax import lax
from jax.experimental import pallas as pl
from jax.experimental.pallas import tpu as pltpu

F32 = jnp.float32
BF16 = jnp.bfloat16

HEAD_DIM = 128
MOBA_BLOCK = 256
MOBA_TOPK = 3
ROPE_THETA = 10000.0
PEER_TOPK = 16
RMS_EPS = 1e-6
ATTN_SCALE = HEAD_DIM ** -0.5
NEG = -1e30
SQRT_HALF = 0.7071067811865476

V7X_VMEM_LIMIT = 56 * 1024 * 1024
ROW_TILE = 256
PEER_EXPERT_TILE = 1024
DECODE_PAGES_PER_STEP = 8

NT = (((1,), (1,)), ((), ()))
TN = (((0,), (0,)), ((), ()))


def _params(*sem):
    return pltpu.CompilerParams(dimension_semantics=sem, vmem_limit_bytes=V7X_VMEM_LIMIT)


def _resident(shape, index_map):
    return pl.BlockSpec(shape, index_map, pipeline_mode=pl.Buffered(1))


def _rms_scale(x, g):
    r = lax.rsqrt(jnp.mean(x * x, axis=-1, keepdims=True) + RMS_EPS)
    return x * r * g


def _log_sigmoid(x):
    return -(jnp.maximum(-x, 0.0) + jnp.log1p(jnp.exp(-jnp.abs(x))))


def _sigmoid(x):
    return 1.0 / (1.0 + jnp.exp(-x))


def _rms_cast_kernel(x_ref, g_ref, o_ref):
    o_ref[...] = _rms_scale(x_ref[...], g_ref[...]).astype(o_ref.dtype)


def rms_cast(x, g):
    m, d = x.shape
    return pl.pallas_call(
        _rms_cast_kernel,
        out_shape=jax.ShapeDtypeStruct((m, d), BF16),
        grid=(m // ROW_TILE,),
        in_specs=[pl.BlockSpec((ROW_TILE, d), lambda i: (i, 0)), pl.BlockSpec((1, d), lambda i: (0, 0))],
        out_specs=pl.BlockSpec((ROW_TILE, d), lambda i: (i, 0)),
        compiler_params=_params("parallel"),
        name="rms_cast",
    )(x, g.reshape(1, d))


def _proj_qkv_kernel(h_ref, w_ref, cos_ref, sin_ref, of_ref, ob_ref, *, n_rope):
    j = pl.program_id(0)
    z = jnp.dot(h_ref[...], w_ref[...], preferred_element_type=F32)

    @pl.when(j < n_rope)
    def _():
        cos = cos_ref[...]
        sin = sin_ref[...]
        for hh in range(z.shape[1] // HEAD_DIM):
            sl = slice(hh * HEAD_DIM, (hh + 1) * HEAD_DIM)
            zh = z[:, sl]
            r = zh * cos + pltpu.roll(zh, HEAD_DIM // 2, 1) * sin
            of_ref[:, sl] = r
            ob_ref[:, sl] = r.astype(BF16)

    @pl.when(j >= n_rope)
    def _():
        of_ref[...] = z
        ob_ref[...] = z.astype(BF16)


def proj_qkv(h, w, cos, sin, seg, n_rope):
    m, d = h.shape
    n_seg = w.shape[1] // seg
    out_spec = pl.BlockSpec((None, ROW_TILE, seg), lambda j, i: (j, i, 0))
    return pl.pallas_call(
        functools.partial(_proj_qkv_kernel, n_rope=n_rope),
        out_shape=(jax.ShapeDtypeStruct((n_seg, m, seg), F32), jax.ShapeDtypeStruct((n_seg, m, seg), BF16)),
        grid=(n_seg, m // ROW_TILE),
        in_specs=[
            pl.BlockSpec((ROW_TILE, d), lambda j, i: (i, 0)),
            pl.BlockSpec((d, seg), lambda j, i: (0, j)),
            pl.BlockSpec((ROW_TILE, HEAD_DIM), lambda j, i: (i, 0)),
            pl.BlockSpec((ROW_TILE, HEAD_DIM), lambda j, i: (i, 0)),
        ],
        out_specs=(out_spec, out_spec),
        compiler_params=_params("parallel", "parallel"),
        name="proj_qkv",
    )(h, w, cos, sin)


def _proj_sigmoid_kernel(h_ref, w_ref, o_ref):
    o_ref[...] = _sigmoid(jnp.dot(h_ref[...], w_ref[...], preferred_element_type=F32))


def proj_sigmoid(h, w, tn):
    m, d = h.shape
    n = w.shape[1]
    return pl.pallas_call(
        _proj_sigmoid_kernel,
        out_shape=jax.ShapeDtypeStruct((m, n), F32),
        grid=(n // tn, m // ROW_TILE),
        in_specs=[pl.BlockSpec((ROW_TILE, d), lambda j, i: (i, 0)), pl.BlockSpec((d, tn), lambda j, i: (0, j))],
        out_specs=pl.BlockSpec((ROW_TILE, tn), lambda j, i: (i, j)),
        compiler_params=_params("parallel", "parallel"),
        name="proj_gates",
    )(h, w)


def _proj_logf_kernel(h_ref, w_ref, b_ref, o_ref):
    o_ref[...] = _log_sigmoid(jnp.dot(h_ref[...], w_ref[...], preferred_element_type=F32) + b_ref[...])


def proj_logf(h, w, b):
    m, d = h.shape
    n = w.shape[1]
    return pl.pallas_call(
        _proj_logf_kernel,
        out_shape=jax.ShapeDtypeStruct((m, n), F32),
        grid=(m // ROW_TILE,),
        in_specs=[pl.BlockSpec((ROW_TILE, d), lambda i: (i, 0)), pl.BlockSpec((d, n), lambda i: (0, 0)),
                  pl.BlockSpec((1, n), lambda i: (0, 0))],
        out_specs=pl.BlockSpec((ROW_TILE, n), lambda i: (i, 0)),
        compiler_params=_params("parallel"),
        name="proj_logf",
    )(h, w, b)


def _cumsum_kernel(lf_ref, col_ref, row_ref, *, n_heads):
    x = lf_ref[...]
    t = x.shape[0]
    row = lax.broadcasted_iota(jnp.int32, x.shape, 0)
    sh = 1
    while sh < t:
        x = x + jnp.where(row >= sh, pltpu.roll(x, sh, 0), 0.0)
        sh *= 2
    col_ref[...] = x
    row_ref[...] = x.T[:n_heads, :]


def cumsum_logf(lf, batch, seq, n_heads):
    lanes = lf.shape[1]
    return pl.pallas_call(
        functools.partial(_cumsum_kernel, n_heads=n_heads),
        out_shape=(jax.ShapeDtypeStruct((batch * seq, lanes), F32), jax.ShapeDtypeStruct((batch, n_heads, seq), F32)),
        grid=(batch,),
        in_specs=[pl.BlockSpec((seq, lanes), lambda b: (b, 0))],
        out_specs=(pl.BlockSpec((seq, lanes), lambda b: (b, 0)), pl.BlockSpec((None, n_heads, seq), lambda b: (b, 0, 0))),
        compiler_params=_params("parallel"),
        name="cumsum_logf",
    )(lf)


def _softmax_pv(s, v):
    m = jnp.max(s, axis=1, keepdims=True)
    p = jnp.exp(s - m)
    l = jnp.sum(p, axis=1, keepdims=True)
    return jnp.dot(p.astype(BF16), v, preferred_element_type=F32) / l


def _causal_block(s_own):
    row = lax.broadcasted_iota(jnp.int32, s_own.shape, 0)
    col = lax.broadcasted_iota(jnp.int32, s_own.shape, 1)
    return jnp.where(col <= row, s_own, NEG)


def _moba_prompt_kernel(q_ref, k_ref, v_ref, kf_ref, o_ref, *, n_blocks):
    qi = pl.program_id(2)
    q = q_ref[...]

    def attend(n_past):
        keys = (n_past + 1) * MOBA_BLOCK
        s = lax.dot_general(q, k_ref[:keys, :], NT, preferred_element_type=F32) * ATTN_SCALE
        pieces = []
        if n_past > 0:
            means = jnp.mean(kf_ref[...].reshape(n_blocks, MOBA_BLOCK, HEAD_DIM), axis=1)
            gate = lax.dot_general(q, means.astype(BF16), NT, preferred_element_type=F32)
            n_iota = lax.broadcasted_iota(jnp.int32, gate.shape, 1)
            rank = jnp.zeros(gate.shape, jnp.int32)
            for m in range(n_past):
                gm = gate[:, m:m + 1]
                rank = rank + jnp.where(gm > gate, 1, jnp.where(gm == gate, jnp.where(m < n_iota, 1, 0), 0))
            chosen = jnp.where(rank < MOBA_TOPK, 1.0, 0.0)
            for n in range(n_past):
                pieces.append(jnp.where(chosen[:, n:n + 1] > 0.5, s[:, n * MOBA_BLOCK:(n + 1) * MOBA_BLOCK], NEG))
        pieces.append(_causal_block(s[:, n_past * MOBA_BLOCK:]))
        o_ref[...] = _softmax_pv(jnp.concatenate(pieces, axis=1), v_ref[:keys, :]).astype(o_ref.dtype)

    for n_past in range(n_blocks):
        pl.when(qi == n_past)(functools.partial(attend, n_past))


def moba_prompt(zb, zf, batch, seq, n_heads):
    m, w = zb.shape[1:]
    nq = seq // MOBA_BLOCK
    kv_map = lambda seg: (lambda b, h, qi: (seg, b, h))
    return pl.pallas_call(
        functools.partial(_moba_prompt_kernel, n_blocks=nq),
        out_shape=jax.ShapeDtypeStruct((m, w), BF16),
        grid=(batch, n_heads, nq),
        in_specs=[
            pl.BlockSpec((None, MOBA_BLOCK, HEAD_DIM), lambda b, h, qi: (0, b * nq + qi, h)),
            pl.BlockSpec((None, seq, HEAD_DIM), kv_map(1)),
            pl.BlockSpec((None, seq, HEAD_DIM), kv_map(2)),
            pl.BlockSpec((None, seq, HEAD_DIM), kv_map(1)),
        ],
        out_specs=pl.BlockSpec((MOBA_BLOCK, HEAD_DIM), lambda b, h, qi: (b * nq + qi, h)),
        compiler_params=_params("parallel", "parallel", "parallel"),
        name="moba_prompt",
    )(zb, zb, zb, zf)


def _fox_prompt_kernel(q_ref, k_ref, v_ref, ccol_ref, crow_ref, o_ref):
    h = pl.program_id(1)
    qi = pl.program_id(2)
    q = q_ref[...]
    tq = q.shape[0]
    ccol = ccol_ref[...]
    lane = lax.broadcasted_iota(jnp.int32, ccol.shape, 1)
    cq = jnp.sum(jnp.where(lane == h, ccol, 0.0), axis=1, keepdims=True)

    def attend(n_past):
        past = n_past * tq
        s = lax.dot_general(q, k_ref[:past + tq, :], NT, preferred_element_type=F32) * ATTN_SCALE
        s = s + (cq - crow_ref[pl.ds(h, 1), :past + tq])
        pieces = [s[:, :past]] if n_past > 0 else []
        pieces.append(_causal_block(s[:, past:]))
        o_ref[...] = _softmax_pv(jnp.concatenate(pieces, axis=1), v_ref[:past + tq, :]).astype(o_ref.dtype)

    for n_past in range(k_ref.shape[0] // tq):
        pl.when(qi == n_past)(functools.partial(attend, n_past))


def fox_prompt(zb, cum_col, cum_row, batch, seq, n_heads, tq):
    m, w = zb.shape[1:]
    nq = seq // tq
    kv_map = lambda seg: (lambda b, h, qi: (seg, b, h))
    return pl.pallas_call(
        _fox_prompt_kernel,
        out_shape=jax.ShapeDtypeStruct((m, w), BF16),
        grid=(batch, n_heads, nq),
        in_specs=[
            pl.BlockSpec((None, tq, HEAD_DIM), lambda b, h, qi: (3, b * nq + qi, h)),
            pl.BlockSpec((None, seq, HEAD_DIM), kv_map(4)),
            pl.BlockSpec((None, seq, HEAD_DIM), kv_map(5)),
            pl.BlockSpec((tq, cum_col.shape[1]), lambda b, h, qi: (b * nq + qi, 0)),
            pl.BlockSpec((None, n_heads, seq), lambda b, h, qi: (b, 0, 0)),
        ],
        out_specs=pl.BlockSpec((tq, HEAD_DIM), lambda b, h, qi: (b * nq + qi, h)),
        compiler_params=_params("parallel", "parallel", "parallel"),
        name="fox_prompt",
    )(zb, zb, zb, cum_col, cum_row)


def _own_head(shape, n_heads):
    row = lax.broadcasted_iota(jnp.int32, shape, 0)
    col = lax.broadcasted_iota(jnp.int32, shape, 1)
    return (col & (n_heads - 1)) == row, col


def _page_rows(ref):
    page, n_heads, hd = ref.shape
    return ref[...].reshape(page * n_heads, hd).astype(BF16)


def _page_spec(cache, index_map):
    page, n_heads, hd = cache.shape[2:]
    return pl.BlockSpec((None, None, page, n_heads, hd), index_map)


def _moba_gate_kernel(pt_ref, q_ref, *refs, pages, n_blocks, n_heads):
    del pt_ref
    k_refs = refs[:pages]
    o_ref = refs[pages]
    means_ref = refs[pages + 1]
    j = pl.program_id(1)
    per_block = MOBA_BLOCK // k_refs[0].shape[0]
    blocks_per_step = pages // per_block
    for r in range(blocks_per_step):
        tot = jnp.sum(k_refs[r * per_block][...], axis=0)
        for t in range(1, per_block):
            tot = tot + jnp.sum(k_refs[r * per_block + t][...], axis=0)
        means_ref[j * blocks_per_step + r] = tot * (1.0 / MOBA_BLOCK)

    @pl.when(j == pl.num_programs(1) - 1)
    def _():
        means = means_ref[...].reshape(n_blocks * n_heads, HEAD_DIM).astype(BF16)
        gate = lax.dot_general(q_ref[...], means, NT, preferred_element_type=F32)
        own, col = _own_head(gate.shape, n_heads)
        gate = jnp.where(own, gate, -jnp.inf)
        blk = lax.shift_right_logical(col, int(math.log2(n_heads))).astype(F32)
        lane = lax.broadcasted_iota(jnp.int32, o_ref.shape, 1)
        out = jnp.full(o_ref.shape, -1.0, F32)
        for r in range(MOBA_TOPK):
            mx = jnp.max(gate, axis=1, keepdims=True)
            ix = jnp.min(jnp.where(gate == mx, blk, float(n_blocks)), axis=1, keepdims=True)
            keep = jnp.where(jnp.abs(mx) < jnp.inf, ix, -1.0)
            out = jnp.where(lane == r, keep, out)
            gate = jnp.where(blk == ix, -jnp.inf, gate)
        o_ref[...] = out.astype(jnp.int32)


def moba_decode_select(page_table, q_heads, cache_k, layer):
    bsz, n_pages = page_table.shape
    page, n_heads, hd = cache_k.shape[2:]
    pages = DECODE_PAGES_PER_STEP
    n_blocks = n_pages * page // MOBA_BLOCK

    def k_map(r):
        return lambda b, j, pt: (pt[b, j * pages + r], layer, 0, 0, 0)

    grid_spec = pltpu.PrefetchScalarGridSpec(
        num_scalar_prefetch=1,
        grid=(bsz, n_pages // pages),
        in_specs=[pl.BlockSpec((None, n_heads, hd), lambda b, j, pt: (b, 0, 0))]
        + [_page_spec(cache_k, k_map(r)) for r in range(pages)],
        out_specs=pl.BlockSpec((None, n_heads, 128), lambda b, j, pt: (b, 0, 0)),
        scratch_shapes=[pltpu.VMEM((n_blocks, n_heads, hd), F32)],
    )
    return pl.pallas_call(
        functools.partial(_moba_gate_kernel, pages=pages, n_blocks=n_blocks, n_heads=n_heads),
        out_shape=jax.ShapeDtypeStruct((bsz, n_heads, 128), jnp.int32),
        grid_spec=grid_spec,
        compiler_params=_params("parallel", "arbitrary"),
        name="moba_decode_select",
    )(page_table, q_heads, *([cache_k] * pages))


def _moba_decode_attn_kernel(pt_ref, sel_ref, q_ref, kn_ref, vn_ref, *refs, n_sel, per_block, n_heads):
    del pt_ref
    n = n_sel * per_block
    k_refs = refs[:n]
    v_refs = refs[n:2 * n]
    o_ref = refs[2 * n]
    b = pl.program_id(0)
    h = pl.program_id(1)
    q = q_ref[...]
    width = k_refs[0].shape[0] * n_heads
    own, _ = _own_head((n_heads, width), n_heads)
    row = lax.broadcasted_iota(jnp.int32, (n_heads, width), 0)
    this_head = lax.broadcasted_iota(jnp.int32, (n_heads, HEAD_DIM), 0) == h
    qk_new = jnp.where(this_head, q.astype(F32) * kn_ref[...].astype(F32), 0.0)
    s_self = jnp.sum(jnp.sum(qk_new, axis=1, keepdims=True), axis=0, keepdims=True) * ATTN_SCALE
    v_new = jnp.sum(jnp.where(this_head, vn_ref[...].astype(F32), 0.0), axis=0, keepdims=True)
    scores = []
    for r in range(n_sel):
        valid = sel_ref[(b * n_heads + h) * n_sel + r] >= 0
        for t in range(per_block):
            s = lax.dot_general(q, _page_rows(k_refs[r * per_block + t]), NT, preferred_element_type=F32) * ATTN_SCALE
            s = jnp.where(row == h, jnp.where(own, s, NEG), NEG)
            scores.append(jnp.where(valid, jnp.max(s, axis=0, keepdims=True), NEG))
    s_all = jnp.concatenate(scores, axis=1)
    m = jnp.maximum(jnp.max(s_all, axis=1, keepdims=True), s_self)
    p = jnp.exp(s_all - m)
    p_self = jnp.exp(s_self - m)
    l = jnp.sum(p, axis=1, keepdims=True) + p_self
    acc = p_self.astype(BF16).astype(F32) * v_new
    for i in range(n):
        p_i = jnp.broadcast_to(p[:, i * width:(i + 1) * width], (n_heads, width)).astype(BF16)
        acc = acc + jnp.dot(p_i, _page_rows(v_refs[i]), preferred_element_type=F32)[0:1, :]
    o_ref[...] = (acc / l).astype(o_ref.dtype)


def moba_decode_attend(page_table, sel, q_heads, k_new, v_new, cache_k, cache_v, layer):
    bsz, n_pages = page_table.shape
    page, n_heads, hd = cache_k.shape[2:]
    per_block = MOBA_BLOCK // page
    n_sel = sel.shape[-1]

    def kv_map(r, t):
        def index(b, h, pt, sl):
            blk = jnp.maximum(sl[(b * n_heads + h) * n_sel + r], 0)
            return (pt[b * n_pages + blk * per_block + t], layer, 0, 0, 0)
        return index

    heads = pl.BlockSpec((None, n_heads, hd), lambda b, h, pt, sl: (b, 0, 0))
    n = n_sel * per_block
    grid_spec = pltpu.PrefetchScalarGridSpec(
        num_scalar_prefetch=2,
        grid=(bsz, n_heads),
        in_specs=[heads, heads, heads]
        + [_page_spec(cache_k, kv_map(r, t)) for r in range(n_sel) for t in range(per_block)]
        + [_page_spec(cache_v, kv_map(r, t)) for r in range(n_sel) for t in range(per_block)],
        out_specs=pl.BlockSpec((None, None, 1, hd), lambda b, h, pt, sl: (b, h, 0, 0)),
    )
    out = pl.pallas_call(
        functools.partial(_moba_decode_attn_kernel, n_sel=n_sel, per_block=per_block, n_heads=n_heads),
        out_shape=jax.ShapeDtypeStruct((bsz, n_heads, 1, hd), BF16),
        grid_spec=grid_spec,
        compiler_params=_params("parallel", "parallel"),
        name="moba_decode_attend",
    )(page_table.reshape(-1), sel.reshape(-1), q_heads, k_new, v_new, *([cache_k] * n), *([cache_v] * n))
    return out.reshape(bsz, n_heads * hd)


def _fox_decode_kernel(pt_ref, q_ref, kn_ref, vn_ref, lfn_ref, *refs, pages, n_heads):
    del pt_ref
    k_refs = refs[:pages]
    v_refs = refs[pages:2 * pages]
    lf_refs = refs[2 * pages:3 * pages]
    o_ref = refs[3 * pages]
    m_ref, l_ref, carry_ref, acc_ref = refs[3 * pages + 1:]
    j = pl.program_id(1)
    q = q_ref[...]
    width = carry_ref.shape[1]
    own, col = _own_head((n_heads, width), n_heads)
    col1 = col[0:1, :]

    @pl.when(j == 0)
    def _():
        s_self = jnp.sum(q.astype(F32) * kn_ref[...].astype(F32), axis=1, keepdims=True) * ATTN_SCALE
        m_ref[...] = jnp.broadcast_to(s_self, m_ref.shape)
        l_ref[...] = jnp.ones(l_ref.shape, F32)
        acc_ref[...] = vn_ref[...].astype(F32)
        lanes = lfn_ref.shape[1]
        t = jnp.where(lax.broadcasted_iota(jnp.int32, (1, lanes), 1) < n_heads, lfn_ref[...], 0.0)
        sh = n_heads
        while sh < lanes:
            t = t + pltpu.roll(t, sh, 1)
            sh *= 2
        carry_ref[...] = jnp.concatenate([t] * (width // lanes), axis=1)

    carry = carry_ref[...]
    scores = []
    for r in range(pages):
        lf = lf_refs[r][...]
        suf = lf
        tot = lf
        sh = n_heads
        while sh < width:
            suf = suf + jnp.where(col1 + sh < width, pltpu.roll(suf, width - sh, 1), 0.0)
            tot = tot + pltpu.roll(tot, sh, 1)
            sh *= 2
        bias = carry + (suf - lf)
        carry = carry + tot
        s = lax.dot_general(q, _page_rows(k_refs[r]), NT, preferred_element_type=F32) * ATTN_SCALE + bias
        scores.append(jnp.where(own, s, NEG))
    carry_ref[...] = carry

    m_old = m_ref[...]
    mx = jnp.max(scores[0], axis=1, keepdims=True)
    for s in scores[1:]:
        mx = jnp.maximum(mx, jnp.max(s, axis=1, keepdims=True))
    m_new = jnp.maximum(m_old, mx)
    alpha = jnp.exp(m_old - m_new)
    l_new = alpha * l_ref[...]
    acc = alpha[:, 0:1] * acc_ref[...]
    for r in range(pages):
        p = jnp.exp(scores[r] - m_new[:, 0:1])
        l_new = l_new + jnp.sum(p, axis=1, keepdims=True)
        acc = acc + jnp.dot(p.astype(BF16), _page_rows(v_refs[r]), preferred_element_type=F32)
    m_ref[...] = m_new
    l_ref[...] = l_new
    acc_ref[...] = acc

    @pl.when(j == pl.num_programs(1) - 1)
    def _():
        o_ref[...] = (acc_ref[...] / l_ref[:, 0:1]).astype(o_ref.dtype)


def fox_decode(page_table, q_heads, k_new, v_new, lf_new, cache_k, cache_v, cache_lf, layer):
    bsz, n_pages = page_table.shape
    page, n_heads, hd = cache_k.shape[2:]
    pages = DECODE_PAGES_PER_STEP
    width = page * n_heads

    def page_map(r):
        return lambda b, j, pt: (pt[b, n_pages - 1 - (j * pages + r)], layer, 0, 0, 0)

    def lf_map(r):
        return lambda b, j, pt: (pt[b, n_pages - 1 - (j * pages + r)], layer, 0, 0)

    heads = pl.BlockSpec((None, n_heads, hd), lambda b, j, pt: (b, 0, 0))
    grid_spec = pltpu.PrefetchScalarGridSpec(
        num_scalar_prefetch=1,
        grid=(bsz, n_pages // pages),
        in_specs=[heads, heads, heads, pl.BlockSpec((None, 1, lf_new.shape[-1]), lambda b, j, pt: (b, 0, 0))]
        + [_page_spec(cache_k, page_map(r)) for r in range(pages)]
        + [_page_spec(cache_v, page_map(r)) for r in range(pages)]
        + [pl.BlockSpec((None, None, 1, width), lf_map(r)) for r in range(pages)],
        out_specs=heads,
        scratch_shapes=[pltpu.VMEM((n_heads, 128), F32), pltpu.VMEM((n_heads, 128), F32),
                        pltpu.VMEM((1, width), F32), pltpu.VMEM((n_heads, hd), F32)],
    )
    out = pl.pallas_call(
        functools.partial(_fox_decode_kernel, pages=pages, n_heads=n_heads),
        out_shape=jax.ShapeDtypeStruct((bsz, n_heads, hd), BF16),
        grid_spec=grid_spec,
        compiler_params=_params("parallel", "arbitrary"),
        name="fox_decode",
    )(page_table, q_heads, k_new, v_new, lf_new,
      *([cache_k] * pages), *([cache_v] * pages), *([cache_lf] * pages))
    return out.reshape(bsz, n_heads * hd)


def _merge_kernel(oa_ref, of_ref, ga_ref, gb_ref, wa_ref, wb_ref, wo_ref, x_ref, g_ref, x1_ref, xn_ref):
    ya = jnp.dot(oa_ref[...], wa_ref[...], preferred_element_type=F32)
    yf = jnp.dot(of_ref[...], wb_ref[...], preferred_element_type=F32)
    mix = ga_ref[...] * ya + gb_ref[...] * yf
    x1 = x_ref[...] + jnp.dot(mix.astype(BF16), wo_ref[...], preferred_element_type=F32)
    x1_ref[...] = x1
    xn_ref[...] = _rms_scale(x1, g_ref[...]).astype(BF16)


def merge_out(oa, of, gates, wa, wb, wo, x, g_next):
    m, d = x.shape
    wdt = oa.shape[1]
    row = lambda width: pl.BlockSpec((ROW_TILE, width), lambda i: (i, 0))
    return pl.pallas_call(
        _merge_kernel,
        out_shape=(jax.ShapeDtypeStruct((m, d), F32), jax.ShapeDtypeStruct((m, d), BF16)),
        grid=(m // ROW_TILE,),
        in_specs=[row(wdt), row(wdt), row(d), pl.BlockSpec((ROW_TILE, d), lambda i: (i, 1)),
                  _resident(wa.shape, lambda i: (0, 0)), _resident(wb.shape, lambda i: (0, 0)),
                  _resident(wo.shape, lambda i: (0, 0)), row(d), _resident((1, d), lambda i: (0, 0))],
        out_specs=(row(d), row(d)),
        compiler_params=_params("parallel"),
        name="merge_out",
    )(oa, of, gates, gates, wa, wb, wo, x, g_next.reshape(1, d))


def _matmul_bf16_kernel(a_ref, w_ref, o_ref):
    o_ref[...] = jnp.dot(a_ref[...], w_ref[...], preferred_element_type=F32).astype(o_ref.dtype)


def matmul_bf16(a, w, tn):
    m, d = a.shape
    n = w.shape[1]
    return pl.pallas_call(
        _matmul_bf16_kernel,
        out_shape=jax.ShapeDtypeStruct((m, n), BF16),
        grid=(n // tn, m // ROW_TILE),
        in_specs=[pl.BlockSpec((ROW_TILE, d), lambda j, i: (i, 0)), pl.BlockSpec((d, tn), lambda j, i: (0, j))],
        out_specs=pl.BlockSpec((ROW_TILE, tn), lambda j, i: (i, j)),
        compiler_params=_params("parallel", "parallel"),
        name="peer_query",
    )(a, w)


def _top_values(x, k):
    vals = []
    for _ in range(k):
        mx = jnp.max(x, axis=0, keepdims=True)
        vals.append(mx)
        x = jnp.where(x == mx, NEG, x)
    return vals


def _peer_route_kernel(q_ref, k1_ref, k2_ref, s1_ref, s2_ref, e1_ref, e2_ref, thr_ref, *, n_heads):
    dk = k1_ref.shape[1]
    k1 = k1_ref[...]
    k2 = k2_ref[...]
    for h in range(n_heads):
        q1 = q_ref[:, (2 * h) * dk:(2 * h + 1) * dk]
        q2 = q_ref[:, (2 * h + 1) * dk:(2 * h + 2) * dk]
        s1 = lax.dot_general(k1, q1, NT, preferred_element_type=F32)
        s2 = lax.dot_general(k2, q2, NT, preferred_element_type=F32)
        t1 = _top_values(s1, PEER_TOPK)
        t2 = _top_values(s2, PEER_TOPK)
        t2_all = jnp.concatenate(t2, axis=0)
        cand = jnp.concatenate([t1[a] + t2_all for a in range(PEER_TOPK)], axis=0)
        top = _top_values(cand, PEER_TOPK)
        z = jnp.zeros_like(top[0])
        for val in top:
            z = z + jnp.exp(val - top[0])
        s1_ref[h] = s1
        s2_ref[h] = s2
        e1_ref[h] = jnp.exp(s1 - t1[0])
        e2_ref[h] = jnp.exp(s2 - t2[0]) / z
        thr_ref[pl.ds(h, 1), :] = top[PEER_TOPK - 1]


def peer_route(qp, k1, k2, n_heads):
    m, w = qp.shape
    n_keys = k1.shape[0]
    big = jax.ShapeDtypeStruct((n_heads, n_keys, m), F32)
    big_spec = pl.BlockSpec((n_heads, n_keys, ROW_TILE), lambda i: (0, 0, i))
    return pl.pallas_call(
        functools.partial(_peer_route_kernel, n_heads=n_heads),
        out_shape=(big, big, big, big, jax.ShapeDtypeStruct((n_heads, m), F32)),
        grid=(m // ROW_TILE,),
        in_specs=[pl.BlockSpec((ROW_TILE, w), lambda i: (i, 0)), pl.BlockSpec(k1.shape, lambda i: (0, 0)),
                  pl.BlockSpec(k2.shape, lambda i: (0, 0))],
        out_specs=(big_spec, big_spec, big_spec, big_spec, pl.BlockSpec((n_heads, ROW_TILE), lambda i: (0, i))),
        compiler_params=_params("parallel"),
        name="peer_route",
    )(qp, k1, k2)


def _peer_dense_kernel(xn_ref, u_ref, v_ref, s1_ref, s2_ref, e1_ref, e2_ref, thr_ref, x_ref, o_ref,
                       at0_ref, at1_ref, cg0_ref, cg1_ref, *, n_heads, n_keys, n_halves):
    k = pl.program_id(1)
    th, tm = at0_ref.shape
    groups = th // n_keys
    chunk = 32
    chunks = n_keys // chunk

    @pl.when(k == 0)
    def _():
        o_ref[...] = x_ref[...]
        at1_ref[...] = jnp.zeros(at1_ref.shape, F32)
        cg0_ref[...] = jnp.zeros(cg0_ref.shape, BF16)

    def activations(half, at_ref):
        at_ref[...] = lax.dot_general(u_ref[half * th:(half + 1) * th, :], xn_ref[...], NT,
                                      preferred_element_type=F32)

    def coefficients(t, at_ref, cg_ref):
        t = jnp.clip(t, 0, n_halves - 1)
        for ig in range(groups):
            i = t * groups + ig
            s1_rows = [s1_ref[h, pl.ds(i, 1), :] for h in range(n_heads)]
            e1_rows = [e1_ref[h, pl.ds(i, 1), :] for h in range(n_heads)]
            for c in range(chunks):
                rows = slice(c * chunk, (c + 1) * chunk)
                acc = jnp.zeros((chunk, tm), F32)
                for h in range(n_heads):
                    pair = s2_ref[h, rows, :] + s1_rows[h]
                    wgt = e2_ref[h, rows, :] * e1_rows[h]
                    acc = jnp.where(pair >= thr_ref[h:h + 1, :], acc + wgt, acc)
                a_rows = slice(ig * n_keys + c * chunk, ig * n_keys + (c + 1) * chunk)
                a = at_ref[a_rows, :]
                gelu = 0.5 * a * (1.0 + lax.erf(a * SQRT_HALF))
                cg_ref[a_rows, :] = (acc * gelu).astype(BF16)

    def accumulate(half, cg_ref):
        o_ref[...] += lax.dot_general(cg_ref[...], v_ref[half * th:(half + 1) * th, :], TN,
                                      preferred_element_type=F32)

    activations(0, at0_ref)
    coefficients(2 * k - 1, at1_ref, cg1_ref)
    accumulate(0, cg0_ref)
    activations(1, at1_ref)
    coefficients(2 * k, at0_ref, cg0_ref)
    accumulate(1, cg1_ref)


def peer_dense(xn, u, v, s1, s2, e1, e2, thr, x):
    m, d = xn.shape
    n_exp = u.shape[0]
    n_heads, n_keys = s1.shape[:2]
    te = PEER_EXPERT_TILE
    th = te // 2
    n_tiles = n_exp // te
    big_spec = pl.BlockSpec((n_heads, n_keys, ROW_TILE), lambda i, k: (0, 0, i))
    return pl.pallas_call(
        functools.partial(_peer_dense_kernel, n_heads=n_heads, n_keys=n_keys, n_halves=2 * n_tiles),
        out_shape=jax.ShapeDtypeStruct((m, d), F32),
        grid=(m // ROW_TILE, n_tiles + 1),
        in_specs=[pl.BlockSpec((ROW_TILE, d), lambda i, k: (i, 0)),
                  pl.BlockSpec((te, d), lambda i, k: (jnp.minimum(k, n_tiles - 1), 0)),
                  pl.BlockSpec((te, d), lambda i, k: (jnp.maximum(k - 1, 0), 0)),
                  big_spec, big_spec, big_spec, big_spec,
                  pl.BlockSpec((n_heads, ROW_TILE), lambda i, k: (0, i)),
                  pl.BlockSpec((ROW_TILE, d), lambda i, k: (i, 0))],
        out_specs=pl.BlockSpec((ROW_TILE, d), lambda i, k: (i, 0)),
        scratch_shapes=[pltpu.VMEM((th, ROW_TILE), F32), pltpu.VMEM((th, ROW_TILE), F32),
                        pltpu.VMEM((th, ROW_TILE), BF16), pltpu.VMEM((th, ROW_TILE), BF16)],
        compiler_params=_params("parallel", "arbitrary"),
        name="peer_dense",
    )(xn, u, v, s1, s2, e1, e2, thr, x)


def _ple_kernel(x_ref, p_ref, g_ref, wg_ref, wp_ref, gf_ref, y_ref):
    x = x_ref[...]
    gate = _sigmoid(jnp.dot(_rms_scale(x, g_ref[...]).astype(BF16), wg_ref[...], preferred_element_type=F32))
    emb = jnp.dot(p_ref[...].astype(BF16), wp_ref[...], preferred_element_type=F32)
    y_ref[...] = _rms_scale(x + gate * emb, gf_ref[...])


def ple_final(x, p, g_ple, wg, wp, g_final):
    m, d = x.shape
    row = lambda width: pl.BlockSpec((ROW_TILE, width), lambda i: (i, 0))
    vec = _resident((1, d), lambda i: (0, 0))
    return pl.pallas_call(
        _ple_kernel,
        out_shape=jax.ShapeDtypeStruct((m, d), F32),
        grid=(m // ROW_TILE,),
        in_specs=[row(d), row(p.shape[1]), vec, _resident(wg.shape, lambda i: (0, 0)),
                  _resident(wp.shape, lambda i: (0, 0)), vec],
        out_specs=row(d),
        compiler_params=_params("parallel"),
        name="ple_final",
    )(x, p, g_ple.reshape(1, d), wg, wp, g_final.reshape(1, d))


def _rope_tables(pos):
    half = HEAD_DIM // 2
    inv_freq = ROPE_THETA ** (-jnp.arange(half, dtype=F32) / half)
    ang = pos.astype(F32)[:, None] * inv_freq[None, :]
    cos, sin = jnp.cos(ang), jnp.sin(ang)
    return jnp.concatenate([cos, cos], axis=1), jnp.concatenate([-sin, sin], axis=1)


def _pad_rows(a, rows):
    return jnp.pad(a, ((0, rows - a.shape[0]),) + ((0, 0),) * (a.ndim - 1))


def _token_stage_in(x, pos, g_attn, w_qkv, w_fl, b_fl, w_gates, seg):
    h = rms_cast(x, g_attn)
    cos, sin = _rope_tables(pos)
    zf, zb = proj_qkv(h, w_qkv, cos, sin, seg, n_rope=2)
    gates = proj_sigmoid(h, w_gates, tn=1024)
    lf = proj_logf(h, w_fl, b_fl)
    return zf, zb, gates, lf


def _token_stage_out(x, oa, of, gates, p, wts):
    x1, xn = merge_out(oa, of, gates, wts["wa"], wts["wb"], wts["wo"], x, wts["g_ffn"])
    qp = matmul_bf16(xn, wts["wq"], tn=1024)
    s1, s2, e1, e2, thr = peer_route(qp, wts["k1"], wts["k2"], wts["peer_heads"])
    x2 = peer_dense(xn, wts["u"], wts["v"], s1, s2, e1, e2, thr, x1)
    return ple_final(x2, p, wts["g_ple"], wts["wg"], wts["wp"], wts["g_final"])


def kernel(x_prompt, x_sample, cache_moba_k, cache_moba_v, cache_fox_k, cache_fox_v, cache_fox_logf, page_table, p_prompt, p_sample, g_attn, w_in, b_forget, w_branch_a, w_branch_b, w_out, g_ffn, w_peer_q, peer_subkey_1, peer_subkey_2, peer_u, peer_v, g_ple, w_ple_gate, w_ple_proj, g_final):
    batch, seq, d = x_prompt.shape
    dec_batch, dec_seq, _ = x_sample.shape
    n_pool, depth, page, n_moba, hd = cache_moba_k.shape
    n_fox = cache_fox_k.shape[3]
    n_pages = page_table.shape[1]
    past_len = n_pages * page
    wa_w, wb_w = n_moba * hd, n_fox * hd
    assert depth == 1 and dec_seq == 1 and hd == HEAD_DIM and wa_w == wb_w
    assert n_moba & (n_moba - 1) == 0 and n_fox & (n_fox - 1) == 0
    assert seq % MOBA_BLOCK == 0 and past_len % MOBA_BLOCK == 0 and MOBA_BLOCK % page == 0
    layer = 0
    n_qkv = 3 * wa_w + 3 * wb_w
    peer_heads = w_peer_q.shape[2] // (2 * peer_subkey_1.shape[2])
    lanes = 128

    w_l = w_in[layer]
    w_qkv = w_l[:, :n_qkv].astype(BF16)
    w_fl = jnp.pad(w_l[:, n_qkv:n_qkv + n_fox], ((0, 0), (0, lanes - n_fox))).astype(BF16)
    b_fl = jnp.pad(b_forget[layer], (0, lanes - n_fox)).reshape(1, lanes)
    w_gates = w_l[:, n_qkv + n_fox:].astype(BF16)
    wts = dict(
        wa=w_branch_a[layer].astype(BF16), wb=w_branch_b[layer].astype(BF16), wo=w_out[layer].astype(BF16),
        g_ffn=g_ffn[layer], wq=w_peer_q[layer].astype(BF16),
        k1=peer_subkey_1[layer].astype(BF16), k2=peer_subkey_2[layer].astype(BF16), peer_heads=peer_heads,
        u=peer_u[layer].astype(BF16), v=peer_v[layer].astype(BF16),
        g_ple=g_ple[layer], wg=w_ple_gate[layer].astype(BF16), wp=w_ple_proj[layer].astype(BF16), g_final=g_final,
    )

    m_p = batch * seq
    xp = x_prompt.reshape(m_p, d)
    pos_p = jnp.tile(jnp.arange(seq, dtype=jnp.int32), batch)
    zf_p, zb_p, gates_p, lf_p = _token_stage_in(xp, pos_p, g_attn[layer].reshape(1, d), w_qkv, w_fl, b_fl, w_gates, wa_w)
    cum_col, cum_row = cumsum_logf(lf_p, batch, seq, n_fox)
    oa_p = moba_prompt(zb_p, zf_p, batch, seq, n_moba)
    of_p = fox_prompt(zb_p, cum_col, cum_row, batch, seq, n_fox, tq=MOBA_BLOCK)
    y_p = _token_stage_out(xp, oa_p, of_p, gates_p, p_prompt[layer].reshape(m_p, -1), wts)

    m_s = ROW_TILE * (-(-dec_batch // ROW_TILE))
    xs = _pad_rows(x_sample.reshape(dec_batch, d), m_s)
    pos_s = jnp.full((m_s,), past_len, jnp.int32)
    zf_s, zb_s, gates_s, lf_s = _token_stage_in(xs, pos_s, g_attn[layer].reshape(1, d), w_qkv, w_fl, b_fl, w_gates, wa_w)
    per_head = lambda seg, n: zb_s[seg, :dec_batch].reshape(dec_batch, n, hd)
    qa_s = per_head(0, n_moba)
    sel = moba_decode_select(page_table, qa_s, cache_moba_k, layer)[:, :, :MOBA_TOPK]
    oa_s = moba_decode_attend(page_table, sel, qa_s, per_head(1, n_moba), per_head(2, n_moba),
                              cache_moba_k, cache_moba_v, layer)
    lf_pages = cache_fox_logf.reshape(n_pool, depth, 1, page * n_fox)
    of_s = fox_decode(page_table, per_head(3, n_fox), per_head(4, n_fox), per_head(5, n_fox),
                      lf_s[:dec_batch].reshape(dec_batch, 1, lanes), cache_fox_k, cache_fox_v, lf_pages, layer)
    y_s = _token_stage_out(xs, _pad_rows(oa_s, m_s), _pad_rows(of_s, m_s), gates_s,
                           _pad_rows(p_sample[layer].reshape(dec_batch, -1), m_s), wts)

    def kv_p(a, n):
        return a.reshape(batch, 1, seq, n, hd)

    def kv_s(a, n):
        return a[:dec_batch].reshape(dec_batch, 1, 1, n, hd)

    return (y_p.reshape(batch, seq, d), y_s[:dec_batch].reshape(dec_batch, 1, d),
            kv_p(zf_p[1], n_moba), kv_p(zf_p[2], n_moba), kv_p(zf_p[4], n_fox), kv_p(zf_p[5], n_fox),
            lf_p[:, :n_fox].reshape(batch, 1, seq, n_fox),
            kv_s(zf_s[1], n_moba), kv_s(zf_s[2], n_moba), kv_s(zf_s[4], n_fox), kv_s(zf_s[5], n_fox),
            lf_s[:dec_batch, :n_fox].reshape(dec_batch, 1, 1, n_fox))
```

```python
import functools
import math

import jax
import jax.numpy as jnp
from jax import lax
from jax.experimental import pallas as pl
from jax.experimental.pallas import tpu as pltpu

F32 = jnp.float32
BF16 = jnp.bfloat16

HEAD_DIM = 128
MOBA_BLOCK = 256
MOBA_TOPK = 3
ROPE_THETA = 10000.0
PEER_TOPK = 16
RMS_EPS = 1e-6
ATTN_SCALE = HEAD_DIM ** -0.5
NEG = -1e30
SQRT_HALF = 0.7071067811865476

V7X_VMEM_LIMIT = 56 * 1024 * 1024
ROW_TILE = 256
PEER_EXPERT_TILE = 1024
DECODE_PAGES_PER_STEP = 8

NT = (((1,), (1,)), ((), ()))
TN = (((0,), (0,)), ((), ()))


def _params(*sem):
    return pltpu.CompilerParams(dimension_semantics=sem, vmem_limit_bytes=V7X_VMEM_LIMIT)


def _resident(shape, index_map):
    return pl.BlockSpec(shape, index_map, pipeline_mode=pl.Buffered(1))


def _rms_scale(x, g):
    r = lax.rsqrt(jnp.mean(x * x, axis=-1, keepdims=True) + RMS_EPS)
    return x * r * g


def _log_sigmoid(x):
    return -(jnp.maximum(-x, 0.0) + jnp.log1p(jnp.exp(-jnp.abs(x))))


def _sigmoid(x):
    return 1.0 / (1.0 + jnp.exp(-x))


def _rms_cast_kernel(x_ref, g_ref, o_ref):
    o_ref[...] = _rms_scale(x_ref[...], g_ref[...]).astype(o_ref.dtype)


def rms_cast(x, g):
    m, d = x.shape
    return pl.pallas_call(
        _rms_cast_kernel,
        out_shape=jax.ShapeDtypeStruct((m, d), BF16),
        grid=(m // ROW_TILE,),
        in_specs=[pl.BlockSpec((ROW_TILE, d), lambda i: (i, 0)), pl.BlockSpec((1, d), lambda i: (0, 0))],
        out_specs=pl.BlockSpec((ROW_TILE, d), lambda i: (i, 0)),
        compiler_params=_params("parallel"),
        name="rms_cast",
    )(x, g.reshape(1, d))


def _proj_qkv_kernel(h_ref, w_ref, cos_ref, sin_ref, of_ref, ob_ref, *, n_rope):
    j = pl.program_id(0)
    z = jnp.dot(h_ref[...], w_ref[...], preferred_element_type=F32)

    @pl.when(j < n_rope)
    def _():
        cos = cos_ref[...]
        sin = sin_ref[...]
        for hh in range(z.shape[1] // HEAD_DIM):
            sl = slice(hh * HEAD_DIM, (hh + 1) * HEAD_DIM)
            zh = z[:, sl]
            r = zh * cos + pltpu.roll(zh, HEAD_DIM // 2, 1) * sin
            of_ref[:, sl] = r
            ob_ref[:, sl] = r.astype(BF16)

    @pl.when(j >= n_rope)
    def _():
        of_ref[...] = z
        ob_ref[...] = z.astype(BF16)


def proj_qkv(h, w, cos, sin, seg, n_rope):
    m, d = h.shape
    n_seg = w.shape[1] // seg
    out_spec = pl.BlockSpec((None, ROW_TILE, seg), lambda j, i: (j, i, 0))
    return pl.pallas_call(
        functools.partial(_proj_qkv_kernel, n_rope=n_rope),
        out_shape=(jax.ShapeDtypeStruct((n_seg, m, seg), F32), jax.ShapeDtypeStruct((n_seg, m, seg), BF16)),
        grid=(n_seg, m // ROW_TILE),
        in_specs=[
            pl.BlockSpec((ROW_TILE, d), lambda j, i: (i, 0)),
            pl.BlockSpec((d, seg), lambda j, i: (0, j)),
            pl.BlockSpec((ROW_TILE, HEAD_DIM), lambda j, i: (i, 0)),
            pl.BlockSpec((ROW_TILE, HEAD_DIM), lambda j, i: (i, 0)),
        ],
        out_specs=(out_spec, out_spec),
        compiler_params=_params("parallel", "parallel"),
        name="proj_qkv",
    )(h, w, cos, sin)


def _proj_sigmoid_kernel(h_ref, w_ref, o_ref):
    o_ref[...] = _sigmoid(jnp.dot(h_ref[...], w_ref[...], preferred_element_type=F32))


def proj_sigmoid(h, w, tn):
    m, d = h.shape
    n = w.shape[1]
    return pl.pallas_call(
        _proj_sigmoid_kernel,
        out_shape=jax.ShapeDtypeStruct((m, n), F32),
        grid=(n // tn, m // ROW_TILE),
        in_specs=[pl.BlockSpec((ROW_TILE, d), lambda j, i: (i, 0)), pl.BlockSpec((d, tn), lambda j, i: (0, j))],
        out_specs=pl.BlockSpec((ROW_TILE, tn), lambda j, i: (i, j)),
        compiler_params=_params("parallel", "parallel"),
        name="proj_gates",
    )(h, w)


def _proj_logf_kernel(h_ref, w_ref, b_ref, o_ref):
    o_ref[...] = _log_sigmoid(jnp.dot(h_ref[...], w_ref[...], preferred_element_type=F32) + b_ref[...])


def proj_logf(h, w, b):
    m, d = h.shape
    n = w.shape[1]
    return pl.pallas_call(
        _proj_logf_kernel,
        out_shape=jax.ShapeDtypeStruct((m, n), F32),
        grid=(m // ROW_TILE,),
        in_specs=[pl.BlockSpec((ROW_TILE, d), lambda i: (i, 0)), pl.BlockSpec((d, n), lambda i: (0, 0)),
                  pl.BlockSpec((1, n), lambda i: (0, 0))],
        out_specs=pl.BlockSpec((ROW_TILE, n), lambda i: (i, 0)),
        compiler_params=_params("parallel"),
        name="proj_logf",
    )(h, w, b)


def _cumsum_kernel(lf_ref, col_ref, row_ref, *, n_heads):
    x = lf_ref[...]
    t = x.shape[0]
    row = lax.broadcasted_iota(jnp.int32, x.shape, 0)
    sh = 1
    while sh < t:
        x = x + jnp.where(row >= sh, pltpu.roll(x, sh, 0), 0.0)
        sh *= 2
    col_ref[...] = x
    row_ref[...] = x.T[:n_heads, :]


def cumsum_logf(lf, batch, seq, n_heads):
    lanes = lf.shape[1]
    return pl.pallas_call(
        functools.partial(_cumsum_kernel, n_heads=n_heads),
        out_shape=(jax.ShapeDtypeStruct((batch * seq, lanes), F32), jax.ShapeDtypeStruct((batch, n_heads, seq), F32)),
        grid=(batch,),
        in_specs=[pl.BlockSpec((seq, lanes), lambda b: (b, 0))],
        out_specs=(pl.BlockSpec((seq, lanes), lambda b: (b, 0)), pl.BlockSpec((None, n_heads, seq), lambda b: (b, 0, 0))),
        compiler_params=_params("parallel"),
        name="cumsum_logf",
    )(lf)


def _softmax_pv(s, v):
    m = jnp.max(s, axis=1, keepdims=True)
    p = jnp.exp(s - m)
    l = jnp.sum(p, axis=1, keepdims=True)
    return jnp.dot(p.astype(BF16), v, preferred_element_type=F32) / l


def _causal_block(s_own):
    row = lax.broadcasted_iota(jnp.int32, s_own.shape, 0)
    col = lax.broadcasted_iota(jnp.int32, s_own.shape, 1)
    return jnp.where(col <= row, s_own, NEG)


def _moba_prompt_kernel(q_ref, k_ref, v_ref, kf_ref, o_ref, *, n_blocks):
    qi = pl.program_id(2)
    q = q_ref[...]

    def attend(n_past):
        keys = (n_past + 1) * MOBA_BLOCK
        s = lax.dot_general(q, k_ref[:keys, :], NT, preferred_element_type=F32) * ATTN_SCALE
        pieces = []
        if n_past > 0:
            means = jnp.mean(kf_ref[...].reshape(n_blocks, MOBA_BLOCK, HEAD_DIM), axis=1)
            gate = lax.dot_general(q, means.astype(BF16), NT, preferred_element_type=F32)
            n_iota = lax.broadcasted_iota(jnp.int32, gate.shape, 1)
            rank = jnp.zeros(gate.shape, jnp.int32)
            for m in range(n_past):
                gm = gate[:, m:m + 1]
                rank = rank + jnp.where(gm > gate, 1, jnp.where(gm == gate, jnp.where(m < n_iota, 1, 0), 0))
            chosen = jnp.where(rank < MOBA_TOPK, 1.0, 0.0)
            for n in range(n_past):
                pieces.append(jnp.where(chosen[:, n:n + 1] > 0.5, s[:, n * MOBA_BLOCK:(n + 1) * MOBA_BLOCK], NEG))
        pieces.append(_causal_block(s[:, n_past * MOBA_BLOCK:]))
        o_ref[...] = _softmax_pv(jnp.concatenate(pieces, axis=1), v_ref[:keys, :]).astype(o_ref.dtype)

    for n_past in range(n_blocks):
        pl.when(qi == n_past)(functools.partial(attend, n_past))


def moba_prompt(zb, zf, batch, seq, n_heads):
    m, w = zb.shape[1:]
    nq = seq // MOBA_BLOCK
    kv_map = lambda seg: (lambda b, h, qi: (seg, b, h))
    return pl.pallas_call(
        functools.partial(_moba_prompt_kernel, n_blocks=nq),
        out_shape=jax.ShapeDtypeStruct((m, w), BF16),
        grid=(batch, n_heads, nq),
        in_specs=[
            pl.BlockSpec((None, MOBA_BLOCK, HEAD_DIM), lambda b, h, qi: (0, b * nq + qi, h)),
            pl.BlockSpec((None, seq, HEAD_DIM), kv_map(1)),
            pl.BlockSpec((None, seq, HEAD_DIM), kv_map(2)),
            pl.BlockSpec((None, seq, HEAD_DIM), kv_map(1)),
        ],
        out_specs=pl.BlockSpec((MOBA_BLOCK, HEAD_DIM), lambda b, h, qi: (b * nq + qi, h)),
        compiler_params=_params("parallel", "parallel", "parallel"),
        name="moba_prompt",
    )(zb, zb, zb, zf)


def _fox_prompt_kernel(q_ref, k_ref, v_ref, ccol_ref, crow_ref, o_ref):
    h = pl.program_id(1)
    qi = pl.program_id(2)
    q = q_ref[...]
    tq = q.shape[0]
    ccol = ccol_ref[...]
    lane = lax.broadcasted_iota(jnp.int32, ccol.shape, 1)
    cq = jnp.sum(jnp.where(lane == h, ccol, 0.0), axis=1, keepdims=True)

    def attend(n_past):
        past = n_past * tq
        s = lax.dot_general(q, k_ref[:past + tq, :], NT, preferred_element_type=F32) * ATTN_SCALE
        s = s + (cq - crow_ref[pl.ds(h, 1), :past + tq])
        pieces = [s[:, :past]] if n_past > 0 else []
        pieces.append(_causal_block(s[:, past:]))
        o_ref[...] = _softmax_pv(jnp.concatenate(pieces, axis=1), v_ref[:past + tq, :]).astype(o_ref.dtype)

    for n_past in range(k_ref.shape[0] // tq):
        pl.when(qi == n_past)(functools.partial(attend, n_past))


def fox_prompt(zb, cum_col, cum_row, batch, seq, n_heads, tq):
    m, w = zb.shape[1:]
    nq = seq // tq
    kv_map = lambda seg: (lambda b, h, qi: (seg, b, h))
    return pl.pallas_call(
        _fox_prompt_kernel,
        out_shape=jax.ShapeDtypeStruct((m, w), BF16),
        grid=(batch, n_heads, nq),
        in_specs=[
            pl.BlockSpec((None, tq, HEAD_DIM), lambda b, h, qi: (3, b * nq + qi, h)),
            pl.BlockSpec((None, seq, HEAD_DIM), kv_map(4)),
            pl.BlockSpec((None, seq, HEAD_DIM), kv_map(5)),
            pl.BlockSpec((tq, cum_col.shape[1]), lambda b, h, qi: (b * nq + qi, 0)),
            pl.BlockSpec((None, n_heads, seq), lambda b, h, qi: (b, 0, 0)),
        ],
        out_specs=pl.BlockSpec((tq, HEAD_DIM), lambda b, h, qi: (b * nq + qi, h)),
        compiler_params=_params("parallel", "parallel", "parallel"),
        name="fox_prompt",
    )(zb, zb, zb, cum_col, cum_row)


def _own_head(shape, n_heads):
    row = lax.broadcasted_iota(jnp.int32, shape, 0)
    col = lax.broadcasted_iota(jnp.int32, shape, 1)
    return (col & (n_heads - 1)) == row, col


def _page_rows(ref):
    page, n_heads, hd = ref.shape
    return ref[...].reshape(page * n_heads, hd).astype(BF16)


def _page_spec(cache, index_map):
    page, n_heads, hd = cache.shape[2:]
    return pl.BlockSpec((None, None, page, n_heads, hd), index_map)


def _moba_gate_kernel(pt_ref, q_ref, *refs, pages, n_blocks, n_heads):
    del pt_ref
    k_refs = refs[:pages]
    o_ref = refs[pages]
    means_ref = refs[pages + 1]
    j = pl.program_id(1)
    per_block = MOBA_BLOCK // k_refs[0].shape[0]
    blocks_per_step = pages // per_block
    for r in range(blocks_per_step):
        tot = jnp.sum(k_refs[r * per_block][...], axis=0)
        for t in range(1, per_block):
            tot = tot + jnp.sum(k_refs[r * per_block + t][...], axis=0)
        means_ref[j * blocks_per_step + r] = tot * (1.0 / MOBA_BLOCK)

    @pl.when(j == pl.num_programs(1) - 1)
    def _():
        means = means_ref[...].reshape(n_blocks * n_heads, HEAD_DIM).astype(BF16)
        gate = lax.dot_general(q_ref[...], means, NT, preferred_element_type=F32)
        own, col = _own_head(gate.shape, n_heads)
        gate = jnp.where(own, gate, -jnp.inf)
        blk = lax.shift_right_logical(col, int(math.log2(n_heads))).astype(F32)
        lane = lax.broadcasted_iota(jnp.int32, o_ref.shape, 1)
        out = jnp.full(o_ref.shape, -1.0, F32)
        for r in range(MOBA_TOPK):
            mx = jnp.max(gate, axis=1, keepdims=True)
            ix = jnp.min(jnp.where(gate == mx, blk, float(n_blocks)), axis=1, keepdims=True)
            keep = jnp.where(jnp.abs(mx) < jnp.inf, ix, -1.0)
            out = jnp.where(lane == r, keep, out)
            gate = jnp.where(blk == ix, -jnp.inf, gate)
        o_ref[...] = out.astype(jnp.int32)


def moba_decode_select(page_table, q_heads, cache_k, layer):
    bsz, n_pages = page_table.shape
    page, n_heads, hd = cache_k.shape[2:]
    pages = DECODE_PAGES_PER_STEP
    n_blocks = n_pages * page // MOBA_BLOCK

    def k_map(r):
        return lambda b, j, pt: (pt[b, j * pages + r], layer, 0, 0, 0)

    grid_spec = pltpu.PrefetchScalarGridSpec(
        num_scalar_prefetch=1,
        grid=(bsz, n_pages // pages),
        in_specs=[pl.BlockSpec((None, n_heads, hd), lambda b, j, pt: (b, 0, 0))]
        + [_page_spec(cache_k, k_map(r)) for r in range(pages)],
        out_specs=pl.BlockSpec((None, n_heads, 128), lambda b, j, pt: (b, 0, 0)),
        scratch_shapes=[pltpu.VMEM((n_blocks, n_heads, hd), F32)],
    )
    return pl.pallas_call(
        functools.partial(_moba_gate_kernel, pages=pages, n_blocks=n_blocks, n_heads=n_heads),
        out_shape=jax.ShapeDtypeStruct((bsz, n_heads, 128), jnp.int32),
        grid_spec=grid_spec,
        compiler_params=_params("parallel", "arbitrary"),
        name="moba_decode_select",
    )(page_table, q_heads, *([cache_k] * pages))


def _moba_decode_attn_kernel(pt_ref, sel_ref, q_ref, kn_ref, vn_ref, ck_ref, cv_ref, o_ref, kbuf, vbuf, sem,
                             *, n_sel, per_block, n_heads, n_pages, layer):
    b = pl.program_id(0)
    n = n_sel * per_block
    page = kbuf.shape[3]
    slot = lax.rem(b, 2)

    def copies(req, buf_slot):
        out = []
        for h in range(n_heads):
            for r in range(n_sel):
                blk = jnp.maximum(sel_ref[(req * n_heads + h) * n_sel + r], 0)
                for t in range(per_block):
                    pg = pt_ref[req * n_pages + blk * per_block + t]
                    i = r * per_block + t
                    out.append(pltpu.make_async_copy(ck_ref.at[pg, layer, :, h, :], kbuf.at[buf_slot, h, i],
                                                     sem.at[buf_slot]))
                    out.append(pltpu.make_async_copy(cv_ref.at[pg, layer, :, h, :], vbuf.at[buf_slot, h, i],
                                                     sem.at[buf_slot]))
        return out

    @pl.when(b == 0)
    def _():
        for cp in copies(0, 0):
            cp.start()

    @pl.when(b + 1 < pl.num_programs(0))
    def _():
        for cp in copies(b + 1, 1 - slot):
            cp.start()

    for cp in copies(b, slot):
        cp.wait()

    q = q_ref[...].astype(F32)
    kn = kn_ref[...].astype(F32)
    vn = vn_ref[...].astype(F32)
    outs = []
    for h in range(n_heads):
        qh = q[h:h + 1, :]
        q8 = jnp.broadcast_to(qh, (8, HEAD_DIM)).astype(BF16)
        s_self = jnp.sum(qh * kn[h:h + 1, :], axis=1, keepdims=True) * ATTN_SCALE
        keys = kbuf[slot, h].reshape(n * page, HEAD_DIM).astype(BF16)
        s = lax.dot_general(q8, keys, NT, preferred_element_type=F32) * ATTN_SCALE
        pieces = []
        for r in range(n_sel):
            valid = sel_ref[(b * n_heads + h) * n_sel + r] >= 0
            pieces.append(jnp.where(valid, s[:, r * MOBA_BLOCK:(r + 1) * MOBA_BLOCK], NEG))
        s = jnp.concatenate(pieces, axis=1)
        m = jnp.maximum(jnp.max(s, axis=1, keepdims=True), s_self)
        p = jnp.exp(s - m)
        p_self = jnp.exp(s_self - m)
        l = jnp.sum(p, axis=1, keepdims=True) + p_self
        vals = vbuf[slot, h].reshape(n * page, HEAD_DIM).astype(BF16)
        acc = jnp.dot(p.astype(BF16), vals, preferred_element_type=F32)
        acc = acc + p_self.astype(BF16).astype(F32) * vn[h:h + 1, :]
        outs.append((acc / l)[0:1, :])
    o_ref[...] = jnp.concatenate(outs, axis=1).astype(o_ref.dtype)


def moba_decode_attend(page_table, sel, q_heads, k_new, v_new, cache_k, cache_v, layer):
    bsz, n_pages = page_table.shape
    page, n_heads, hd = cache_k.shape[2:]
    per_block = MOBA_BLOCK // page
    n_sel = sel.shape[-1]
    n = n_sel * per_block
    heads = pl.BlockSpec((None, n_heads, hd), lambda b, pt, sl: (b, 0, 0))
    grid_spec = pltpu.PrefetchScalarGridSpec(
        num_scalar_prefetch=2,
        grid=(bsz,),
        in_specs=[heads, heads, heads, pl.BlockSpec(memory_space=pl.ANY), pl.BlockSpec(memory_space=pl.ANY)],
        out_specs=pl.BlockSpec((None, 1, n_heads * hd), lambda b, pt, sl: (b, 0, 0)),
        scratch_shapes=[pltpu.VMEM((2, n_heads, n, page, hd), cache_k.dtype),
                        pltpu.VMEM((2, n_heads, n, page, hd), cache_v.dtype),
                        pltpu.SemaphoreType.DMA((2,))],
    )
    out = pl.pallas_call(
        functools.partial(_moba_decode_attn_kernel, n_sel=n_sel, per_block=per_block, n_heads=n_heads,
                          n_pages=n_pages, layer=layer),
        out_shape=jax.ShapeDtypeStruct((bsz, 1, n_heads * hd), BF16),
        grid_spec=grid_spec,
        compiler_params=_params("arbitrary"),
        name="moba_decode_attend",
    )(page_table.reshape(-1), sel.reshape(-1), q_heads, k_new, v_new, cache_k, cache_v)
    return out.reshape(bsz, n_heads * hd)


def _fox_decode_kernel(pt_ref, q_ref, kn_ref, vn_ref, lfn_ref, *refs, pages, n_heads):
    del pt_ref
    k_refs = refs[:pages]
    v_refs = refs[pages:2 * pages]
    lf_refs = refs[2 * pages:3 * pages]
    o_ref = refs[3 * pages]
    m_ref, l_ref, carry_ref, acc_ref = refs[3 * pages + 1:]
    j = pl.program_id(1)
    q = q_ref[...]
    width = carry_ref.shape[1]
    own, col = _own_head((n_heads, width), n_heads)
    col1 = col[0:1, :]

    @pl.when(j == 0)
    def _():
        s_self = jnp.sum(q.astype(F32) * kn_ref[...].astype(F32), axis=1, keepdims=True) * ATTN_SCALE
        m_ref[...] = jnp.broadcast_to(s_self, m_ref.shape)
        l_ref[...] = jnp.ones(l_ref.shape, F32)
        acc_ref[...] = vn_ref[...].astype(F32)
        lanes = lfn_ref.shape[1]
        t = jnp.where(lax.broadcasted_iota(jnp.int32, (1, lanes), 1) < n_heads, lfn_ref[...], 0.0)
        sh = n_heads
        while sh < lanes:
            t = t + pltpu.roll(t, sh, 1)
            sh *= 2
        carry_ref[...] = jnp.concatenate([t] * (width // lanes), axis=1)

    carry = carry_ref[...]
    scores = []
    for r in range(pages):
        lf = lf_refs[r][...]
        suf = lf
        tot = lf
        sh = n_heads
        while sh < width:
            suf = suf + jnp.where(col1 + sh < width, pltpu.roll(suf, width - sh, 1), 0.0)
            tot = tot + pltpu.roll(tot, sh, 1)
            sh *= 2
        bias = carry + (suf - lf)
        carry = carry + tot
        s = lax.dot_general(q, _page_rows(k_refs[r]), NT, preferred_element_type=F32) * ATTN_SCALE + bias
        scores.append(jnp.where(own, s, NEG))
    carry_ref[...] = carry

    m_old = m_ref[...]
    mx = jnp.max(scores[0], axis=1, keepdims=True)
    for s in scores[1:]:
        mx = jnp.maximum(mx, jnp.max(s, axis=1, keepdims=True))
    m_new = jnp.maximum(m_old, mx)
    alpha = jnp.exp(m_old - m_new)
    l_new = alpha * l_ref[...]
    acc = alpha[:, 0:1] * acc_ref[...]
    for r in range(pages):
        p = jnp.exp(scores[r] - m_new[:, 0:1])
        l_new = l_new + jnp.sum(p, axis=1, keepdims=True)
        acc = acc + jnp.dot(p.astype(BF16), _page_rows(v_refs[r]), preferred_element_type=F32)
    m_ref[...] = m_new
    l_ref[...] = l_new
    acc_ref[...] = acc

    @pl.when(j == pl.num_programs(1) - 1)
    def _():
        o_ref[...] = (acc_ref[...] / l_ref[:, 0:1]).astype(o_ref.dtype)


def fox_decode(page_table, q_heads, k_new, v_new, lf_new, cache_k, cache_v, cache_lf, layer):
    bsz, n_pages = page_table.shape
    page, n_heads, hd = cache_k.shape[2:]
    pages = DECODE_PAGES_PER_STEP
    width = page * n_heads

    def page_map(r):
        return lambda b, j, pt: (pt[b, n_pages - 1 - (j * pages + r)], layer, 0, 0, 0)

    def lf_map(r):
        return lambda b, j, pt: (pt[b, n_pages - 1 - (j * pages + r)], layer, 0, 0)

    heads = pl.BlockSpec((None, n_heads, hd), lambda b, j, pt: (b, 0, 0))
    grid_spec = pltpu.PrefetchScalarGridSpec(
        num_scalar_prefetch=1,
        grid=(bsz, n_pages // pages),
        in_specs=[heads, heads, heads, pl.BlockSpec((None, 1, lf_new.shape[-1]), lambda b, j, pt: (b, 0, 0))]
        + [_page_spec(cache_k, page_map(r)) for r in range(pages)]
        + [_page_spec(cache_v, page_map(r)) for r in range(pages)]
        + [pl.BlockSpec((None, None, 1, width), lf_map(r)) for r in range(pages)],
        out_specs=heads,
        scratch_shapes=[pltpu.VMEM((n_heads, 128), F32), pltpu.VMEM((n_heads, 128), F32),
                        pltpu.VMEM((1, width), F32), pltpu.VMEM((n_heads, hd), F32)],
    )
    out = pl.pallas_call(
        functools.partial(_fox_decode_kernel, pages=pages, n_heads=n_heads),
        out_shape=jax.ShapeDtypeStruct((bsz, n_heads, hd), BF16),
        grid_spec=grid_spec,
        compiler_params=_params("parallel", "arbitrary"),
        name="fox_decode",
    )(page_table, q_heads, k_new, v_new, lf_new,
      *([cache_k] * pages), *([cache_v] * pages), *([cache_lf] * pages))
    return out.reshape(bsz, n_heads * hd)


def _merge_kernel(oa_ref, of_ref, ga_ref, gb_ref, wa_ref, wb_ref, wo_ref, x_ref, g_ref, x1_ref, xn_ref):
    ya = jnp.dot(oa_ref[...], wa_ref[...], preferred_element_type=F32)
    yf = jnp.dot(of_ref[...], wb_ref[...], preferred_element_type=F32)
    mix = ga_ref[...] * ya + gb_ref[...] * yf
    x1 = x_ref[...] + jnp.dot(mix.astype(BF16), wo_ref[...], preferred_element_type=F32)
    x1_ref[...] = x1
    xn_ref[...] = _rms_scale(x1, g_ref[...]).astype(BF16)


def merge_out(oa, of, gates, wa, wb, wo, x, g_next):
    m, d = x.shape
    wdt = oa.shape[1]
    row = lambda width: pl.BlockSpec((ROW_TILE, width), lambda i: (i, 0))
    return pl.pallas_call(
        _merge_kernel,
        out_shape=(jax.ShapeDtypeStruct((m, d), F32), jax.ShapeDtypeStruct((m, d), BF16)),
        grid=(m // ROW_TILE,),
        in_specs=[row(wdt), row(wdt), row(d), pl.BlockSpec((ROW_TILE, d), lambda i: (i, 1)),
                  _resident(wa.shape, lambda i: (0, 0)), _resident(wb.shape, lambda i: (0, 0)),
                  _resident(wo.shape, lambda i: (0, 0)), row(d), _resident((1, d), lambda i: (0, 0))],
        out_specs=(row(d), row(d)),
        compiler_params=_params("parallel"),
        name="merge_out",
    )(oa, of, gates, gates, wa, wb, wo, x, g_next.reshape(1, d))


def _matmul_bf16_kernel(a_ref, w_ref, o_ref):
    o_ref[...] = jnp.dot(a_ref[...], w_ref[...], preferred_element_type=F32).astype(o_ref.dtype)


def matmul_bf16(a, w, tn):
    m, d = a.shape
    n = w.shape[1]
    return pl.pallas_call(
        _matmul_bf16_kernel,
        out_shape=jax.ShapeDtypeStruct((m, n), BF16),
        grid=(n // tn, m // ROW_TILE),
        in_specs=[pl.BlockSpec((ROW_TILE, d), lambda j, i: (i, 0)), pl.BlockSpec((d, tn), lambda j, i: (0, j))],
        out_specs=pl.BlockSpec((ROW_TILE, tn), lambda j, i: (i, j)),
        compiler_params=_params("parallel", "parallel"),
        name="peer_query",
    )(a, w)


def _top_values(x, k):
    vals = []
    for _ in range(k):
        mx = jnp.max(x, axis=0, keepdims=True)
        vals.append(mx)
        x = jnp.where(x == mx, NEG, x)
    return vals


def _peer_route_kernel(q_ref, k1_ref, k2_ref, s1_ref, s2_ref, e1_ref, e2_ref, thr_ref, *, n_heads):
    dk = k1_ref.shape[1]
    k1 = k1_ref[...]
    k2 = k2_ref[...]
    for h in range(n_heads):
        q1 = q_ref[:, (2 * h) * dk:(2 * h + 1) * dk]
        q2 = q_ref[:, (2 * h + 1) * dk:(2 * h + 2) * dk]
        s1 = lax.dot_general(k1, q1, NT, preferred_element_type=F32)
        s2 = lax.dot_general(k2, q2, NT, preferred_element_type=F32)
        t1 = _top_values(s1, PEER_TOPK)
        t2 = _top_values(s2, PEER_TOPK)
        t2_all = jnp.concatenate(t2, axis=0)
        cand = jnp.concatenate([t1[a] + t2_all for a in range(PEER_TOPK)], axis=0)
        top = _top_values(cand, PEER_TOPK)
        z = jnp.zeros_like(top[0])
        for val in top:
            z = z + jnp.exp(val - top[0])
        s1_ref[h] = s1
        s2_ref[h] = s2
        e1_ref[h] = jnp.exp(s1 - t1[0])
        e2_ref[h] = jnp.exp(s2 - t2[0]) / z
        thr_ref[pl.ds(h, 1), :] = top[PEER_TOPK - 1]


def peer_route(qp, k1, k2, n_heads):
    m, w = qp.shape
    n_keys = k1.shape[0]
    big = jax.ShapeDtypeStruct((n_heads, n_keys, m), F32)
    big_spec = pl.BlockSpec((n_heads, n_keys, ROW_TILE), lambda i: (0, 0, i))
    return pl.pallas_call(
        functools.partial(_peer_route_kernel, n_heads=n_heads),
        out_shape=(big, big, big, big, jax.ShapeDtypeStruct((n_heads, m), F32)),
        grid=(m // ROW_TILE,),
        in_specs=[pl.BlockSpec((ROW_TILE, w), lambda i: (i, 0)), pl.BlockSpec(k1.shape, lambda i: (0, 0)),
                  pl.BlockSpec(k2.shape, lambda i: (0, 0))],
        out_specs=(big_spec, big_spec, big_spec, big_spec, pl.BlockSpec((n_heads, ROW_TILE), lambda i: (0, i))),
        compiler_params=_params("parallel"),
        name="peer_route",
    )(qp, k1, k2)


def _peer_dense_kernel(xn_ref, u_ref, v_ref, s1p_ref, s1c_ref, s2_ref, e1p_ref, e1c_ref, e2_ref, thr_ref, x_ref,
                       o_ref, at0_ref, at1_ref, cg0_ref, cg1_ref, *, n_heads, n_keys):
    k = pl.program_id(1)
    th, tm = at0_ref.shape
    groups = th // n_keys
    chunk = 32
    chunks = n_keys // chunk
    lane_tile = 128

    @pl.when(k == 0)
    def _():
        o_ref[...] = x_ref[...]
        at1_ref[...] = jnp.zeros(at1_ref.shape, F32)
        cg0_ref[...] = jnp.zeros(cg0_ref.shape, BF16)

    def activations(half, at_ref):
        at_ref[...] = lax.dot_general(u_ref[half * th:(half + 1) * th, :], xn_ref[...], NT,
                                      preferred_element_type=F32)

    def coefficients(s1_ref, e1_ref, half, at_ref, cg_ref):
        for lt in range(tm // lane_tile):
            lanes = slice(lt * lane_tile, (lt + 1) * lane_tile)
            for c in range(chunks):
                rows = slice(c * chunk, (c + 1) * chunk)
                accs = [jnp.zeros((chunk, lane_tile), F32) for _ in range(groups)]
                for h in range(n_heads):
                    s2c = s2_ref[h, rows, lanes]
                    e2c = e2_ref[h, rows, lanes]
                    thr = thr_ref[h:h + 1, lanes]
                    for ig in range(groups):
                        r = half * groups + ig
                        pair = s2c + s1_ref[h, r:r + 1, lanes]
                        wgt = e2c * e1_ref[h, r:r + 1, lanes]
                        accs[ig] = jnp.where(pair >= thr, accs[ig] + wgt, accs[ig])
                for ig in range(groups):
                    a_rows = slice(ig * n_keys + c * chunk, ig * n_keys + (c + 1) * chunk)
                    a = at_ref[a_rows, lanes]
                    gelu = 0.5 * a * (1.0 + lax.erf(a * SQRT_HALF))
                    cg_ref[a_rows, lanes] = (accs[ig] * gelu).astype(BF16)

    def accumulate(half, cg_ref):
        o_ref[...] += lax.dot_general(cg_ref[...], v_ref[half * th:(half + 1) * th, :], TN,
                                      preferred_element_type=F32)

    activations(0, at0_ref)
    coefficients(s1p_ref, e1p_ref, 1, at1_ref, cg1_ref)
    accumulate(0, cg0_ref)
    activations(1, at1_ref)
    coefficients(s1c_ref, e1c_ref, 0, at0_ref, cg0_ref)
    accumulate(1, cg1_ref)


def peer_dense(xn, u, v, s1, s2, e1, e2, thr, x):
    m, d = xn.shape
    n_exp = u.shape[0]
    n_heads, n_keys = s1.shape[:2]
    te = PEER_EXPERT_TILE
    th = te // 2
    n_tiles = n_exp // te
    per_tile = te // n_keys
    cur = lambda i, k: (0, jnp.minimum(k, n_tiles - 1), i)
    prev = lambda i, k: (0, jnp.maximum(k - 1, 0), i)
    rows_spec = lambda index_map: pl.BlockSpec((n_heads, per_tile, ROW_TILE), index_map)
    big_spec = pl.BlockSpec((n_heads, n_keys, ROW_TILE), lambda i, k: (0, 0, i))
    return pl.pallas_call(
        functools.partial(_peer_dense_kernel, n_heads=n_heads, n_keys=n_keys),
        out_shape=jax.ShapeDtypeStruct((m, d), F32),
        grid=(m // ROW_TILE, n_tiles + 1),
        in_specs=[pl.BlockSpec((ROW_TILE, d), lambda i, k: (i, 0)),
                  pl.BlockSpec((te, d), lambda i, k: (jnp.minimum(k, n_tiles - 1), 0)),
                  pl.BlockSpec((te, d), lambda i, k: (jnp.maximum(k - 1, 0), 0)),
                  rows_spec(prev), rows_spec(cur), big_spec, rows_spec(prev), rows_spec(cur), big_spec,
                  pl.BlockSpec((n_heads, ROW_TILE), lambda i, k: (0, i)),
                  pl.BlockSpec((ROW_TILE, d), lambda i, k: (i, 0))],
        out_specs=pl.BlockSpec((ROW_TILE, d), lambda i, k: (i, 0)),
        scratch_shapes=[pltpu.VMEM((th, ROW_TILE), F32), pltpu.VMEM((th, ROW_TILE), F32),
                        pltpu.VMEM((th, ROW_TILE), BF16), pltpu.VMEM((th, ROW_TILE), BF16)],
        compiler_params=_params("parallel", "arbitrary"),
        name="peer_dense",
    )(xn, u, v, s1, s1, s2, e1, e1, e2, thr, x)


def _ple_kernel(x_ref, p_ref, g_ref, wg_ref, wp_ref, gf_ref, y_ref):
    x = x_ref[...]
    gate = _sigmoid(jnp.dot(_rms_scale(x, g_ref[...]).astype(BF16), wg_ref[...], preferred_element_type=F32))
    emb = jnp.dot(p_ref[...].astype(BF16), wp_ref[...], preferred_element_type=F32)
    y_ref[...] = _rms_scale(x + gate * emb, gf_ref[...])


def ple_final(x, p, g_ple, wg, wp, g_final):
    m, d = x.shape
    row = lambda width: pl.BlockSpec((ROW_TILE, width), lambda i: (i, 0))
    vec = _resident((1, d), lambda i: (0, 0))
    return pl.pallas_call(
        _ple_kernel,
        out_shape=jax.ShapeDtypeStruct((m, d), F32),
        grid=(m // ROW_TILE,),
        in_specs=[row(d), row(p.shape[1]), vec, _resident(wg.shape, lambda i: (0, 0)),
                  _resident(wp.shape, lambda i: (0, 0)), vec],
        out_specs=row(d),
        compiler_params=_params("parallel"),
        name="ple_final",
    )(x, p, g_ple.reshape(1, d), wg, wp, g_final.reshape(1, d))


def _rope_tables(pos):
    half = HEAD_DIM // 2
    inv_freq = ROPE_THETA ** (-jnp.arange(half, dtype=F32) / half)
    ang = pos.astype(F32)[:, None] * inv_freq[None, :]
    cos, sin = jnp.cos(ang), jnp.sin(ang)
    return jnp.concatenate([cos, cos], axis=1), jnp.concatenate([-sin, sin], axis=1)


def _pad_rows(a, rows):
    return jnp.pad(a, ((0, rows - a.shape[0]),) + ((0, 0),) * (a.ndim - 1))


def _token_stage_in(x, pos, g_attn, w_qkv, w_fl, b_fl, w_gates, seg):
    h = rms_cast(x, g_attn)
    cos, sin = _rope_tables(pos)
    zf, zb = proj_qkv(h, w_qkv, cos, sin, seg, n_rope=2)
    gates = proj_sigmoid(h, w_gates, tn=1024)
    lf = proj_logf(h, w_fl, b_fl)
    return zf, zb, gates, lf


def _token_stage_out(x, oa, of, gates, p, wts):
    x1, xn = merge_out(oa, of, gates, wts["wa"], wts["wb"], wts["wo"], x, wts["g_ffn"])
    qp = matmul_bf16(xn, wts["wq"], tn=1024)
    s1, s2, e1, e2, thr = peer_route(qp, wts["k1"], wts["k2"], wts["peer_heads"])
    x2 = peer_dense(xn, wts["u"], wts["v"], s1, s2, e1, e2, thr, x1)
    return ple_final(x2, p, wts["g_ple"], wts["wg"], wts["wp"], wts["g_final"])


def kernel(x_prompt, x_sample, cache_moba_k, cache_moba_v, cache_fox_k, cache_fox_v, cache_fox_logf, page_table, p_prompt, p_sample, g_attn, w_in, b_forget, w_branch_a, w_branch_b, w_out, g_ffn, w_peer_q, peer_subkey_1, peer_subkey_2, peer_u, peer_v, g_ple, w_ple_gate, w_ple_proj, g_final):
    batch, seq, d = x_prompt.shape
    dec_batch, dec_seq, _ = x_sample.shape
    n_pool, depth, page, n_moba, hd = cache_moba_k.shape
    n_fox = cache_fox_k.shape[3]
    n_pages = page_table.shape[1]
    past_len = n_pages * page
    wa_w, wb_w = n_moba * hd, n_fox * hd
    assert depth == 1 and dec_seq == 1 and hd == HEAD_DIM and wa_w == wb_w
    assert n_moba & (n_moba - 1) == 0 and n_fox & (n_fox - 1) == 0
    assert seq % MOBA_BLOCK == 0 and past_len % MOBA_BLOCK == 0 and MOBA_BLOCK % page == 0
    layer = 0
    n_qkv = 3 * wa_w + 3 * wb_w
    peer_heads = w_peer_q.shape[2] // (2 * peer_subkey_1.shape[2])
    lanes = 128

    w_l = w_in[layer]
    w_qkv = w_l[:, :n_qkv].astype(BF16)
    w_fl = jnp.pad(w_l[:, n_qkv:n_qkv + n_fox], ((0, 0), (0, lanes - n_fox))).astype(BF16)
    b_fl = jnp.pad(b_forget[layer], (0, lanes - n_fox)).reshape(1, lanes)
    w_gates = w_l[:, n_qkv + n_fox:].astype(BF16)
    wts = dict(
        wa=w_branch_a[layer].astype(BF16), wb=w_branch_b[layer].astype(BF16), wo=w_out[layer].astype(BF16),
        g_ffn=g_ffn[layer], wq=w_peer_q[layer].astype(BF16),
        k1=peer_subkey_1[layer].astype(BF16), k2=peer_subkey_2[layer].astype(BF16), peer_heads=peer_heads,
        u=peer_u[layer].astype(BF16), v=peer_v[layer].astype(BF16),
        g_ple=g_ple[layer], wg=w_ple_gate[layer].astype(BF16), wp=w_ple_proj[layer].astype(BF16), g_final=g_final,
    )

    m_p = batch * seq
    xp = x_prompt.reshape(m_p, d)
    pos_p = jnp.tile(jnp.arange(seq, dtype=jnp.int32), batch)
    zf_p, zb_p, gates_p, lf_p = _token_stage_in(xp, pos_p, g_attn[layer].reshape(1, d), w_qkv, w_fl, b_fl, w_gates, wa_w)
    cum_col, cum_row = cumsum_logf(lf_p, batch, seq, n_fox)
    oa_p = moba_prompt(zb_p, zf_p, batch, seq, n_moba)
    of_p = fox_prompt(zb_p, cum_col, cum_row, batch, seq, n_fox, tq=MOBA_BLOCK)
    y_p = _token_stage_out(xp, oa_p, of_p, gates_p, p_prompt[layer].reshape(m_p, -1), wts)

    m_s = ROW_TILE * (-(-dec_batch // ROW_TILE))
    xs = _pad_rows(x_sample.reshape(dec_batch, d), m_s)
    pos_s = jnp.full((m_s,), past_len, jnp.int32)
    zf_s, zb_s, gates_s, lf_s = _token_stage_in(xs, pos_s, g_attn[layer].reshape(1, d), w_qkv, w_fl, b_fl, w_gates, wa_w)
    per_head = lambda seg, n: zb_s[seg, :dec_batch].reshape(dec_batch, n, hd)
    qa_s = per_head(0, n_moba)
    sel = moba_decode_select(page_table, qa_s, cache_moba_k, layer)[:, :, :MOBA_TOPK]
    oa_s = moba_decode_attend(page_table, sel, qa_s, per_head(1, n_moba), per_head(2, n_moba),
                              cache_moba_k, cache_moba_v, layer)
    lf_pages = cache_fox_logf.reshape(n_pool, depth, 1, page * n_fox)
    of_s = fox_decode(page_table, per_head(3, n_fox), per_head(4, n_fox), per_head(5, n_fox),
                      lf_s[:dec_batch].reshape(dec_batch, 1, lanes), cache_fox_k, cache_fox_v, lf_pages, layer)
    y_s = _token_stage_out(xs, _pad_rows(oa_s, m_s), _pad_rows(of_s, m_s), gates_s,
                           _pad_rows(p_sample[layer].reshape(dec_batch, -1), m_s), wts)

    def kv_p(a, n):
        return a.reshape(batch, 1, seq, n, hd)

    def kv_s(a, n):
        return a[:dec_batch].reshape(dec_batch, 1, 1, n, hd)

    return (y_p.reshape(batch, seq, d), y_s[:dec_batch].reshape(dec_batch, 1, d),
            kv_p(zf_p[1], n_moba), kv_p(zf_p[2], n_moba), kv_p(zf_p[4], n_fox), kv_p(zf_p[5], n_fox),
            lf_p[:, :n_fox].reshape(batch, 1, seq, n_fox),
            kv_s(zf_s[1], n_moba), kv_s(zf_s[2], n_moba), kv_s(zf_s[4], n_fox), kv_s(zf_s[5], n_fox),
            lf_s[:dec_batch, :n_fox].reshape(dec_batch, 1, 1, n_fox))
```

```python
import functools
import math

import jax
import jax.numpy as jnp
from jax import lax
from jax.experimental import pallas as pl
from jax.experimental.pallas import tpu as pltpu

F32 = jnp.float32
BF16 = jnp.bfloat16

HEAD_DIM = 128
MOBA_BLOCK = 256
MOBA_TOPK = 3
ROPE_THETA = 10000.0
PEER_TOPK = 16
RMS_EPS = 1e-6
ATTN_SCALE = HEAD_DIM ** -0.5
NEG = -1e30
SQRT_HALF = 0.7071067811865476

V7X_VMEM_LIMIT = 56 * 1024 * 1024
ROW_TILE = 256
PEER_EXPERT_TILE = 1024
DECODE_PAGES_PER_STEP = 8

NT = (((1,), (1,)), ((), ()))
TN = (((0,), (0,)), ((), ()))


def _params(*sem):
    return pltpu.CompilerParams(dimension_semantics=sem, vmem_limit_bytes=V7X_VMEM_LIMIT)


def _resident(shape, index_map):
    return pl.BlockSpec(shape, index_map, pipeline_mode=pl.Buffered(1))


def _rms_scale(x, g):
    r = lax.rsqrt(jnp.mean(x * x, axis=-1, keepdims=True) + RMS_EPS)
    return x * r * g


def _log_sigmoid(x):
    return -(jnp.maximum(-x, 0.0) + jnp.log1p(jnp.exp(-jnp.abs(x))))


def _sigmoid(x):
    return 1.0 / (1.0 + jnp.exp(-x))


def _rms_cast_kernel(x_ref, g_ref, o_ref):
    o_ref[...] = _rms_scale(x_ref[...], g_ref[...]).astype(o_ref.dtype)


def rms_cast(x, g):
    m, d = x.shape
    return pl.pallas_call(
        _rms_cast_kernel,
        out_shape=jax.ShapeDtypeStruct((m, d), BF16),
        grid=(m // ROW_TILE,),
        in_specs=[pl.BlockSpec((ROW_TILE, d), lambda i: (i, 0)), pl.BlockSpec((1, d), lambda i: (0, 0))],
        out_specs=pl.BlockSpec((ROW_TILE, d), lambda i: (i, 0)),
        compiler_params=_params("parallel"),
        name="rms_cast",
    )(x, g.reshape(1, d))


def _proj_heads_kernel(h_ref, w_ref, *refs, rope, keep_f32):
    z = jnp.dot(h_ref[...], w_ref[...], preferred_element_type=F32)
    out_refs = refs[2:] if rope else refs
    if rope:
        cos = refs[0][...]
        sin = refs[1][...]
        parts = []
        for hh in range(z.shape[1] // HEAD_DIM):
            zh = z[:, hh * HEAD_DIM:(hh + 1) * HEAD_DIM]
            parts.append(zh * cos + pltpu.roll(zh, HEAD_DIM // 2, 1) * sin)
        z = jnp.concatenate(parts, axis=1)
    out_refs[0][...] = z.astype(BF16)
    if keep_f32:
        out_refs[1][...] = z


def proj_heads(h, w, cos, sin, rope, keep_f32):
    m, d = h.shape
    n = w.shape[1]
    row = lambda width: pl.BlockSpec((ROW_TILE, width), lambda i: (i, 0))
    out_shape = [jax.ShapeDtypeStruct((m, n), BF16)] + ([jax.ShapeDtypeStruct((m, n), F32)] if keep_f32 else [])
    return pl.pallas_call(
        functools.partial(_proj_heads_kernel, rope=rope, keep_f32=keep_f32),
        out_shape=out_shape,
        grid=(m // ROW_TILE,),
        in_specs=[row(d), _resident((d, n), lambda i: (0, 0))] + ([row(HEAD_DIM), row(HEAD_DIM)] if rope else []),
        out_specs=[row(n)] * len(out_shape),
        compiler_params=_params("parallel"),
        name="proj_heads",
    )(h, w, *((cos, sin) if rope else ()))


def _proj_sigmoid_kernel(h_ref, w_ref, o_ref):
    o_ref[...] = _sigmoid(jnp.dot(h_ref[...], w_ref[...], preferred_element_type=F32))


def proj_sigmoid(h, w, tn):
    m, d = h.shape
    n = w.shape[1]
    return pl.pallas_call(
        _proj_sigmoid_kernel,
        out_shape=jax.ShapeDtypeStruct((m, n), F32),
        grid=(n // tn, m // ROW_TILE),
        in_specs=[pl.BlockSpec((ROW_TILE, d), lambda j, i: (i, 0)), pl.BlockSpec((d, tn), lambda j, i: (0, j))],
        out_specs=pl.BlockSpec((ROW_TILE, tn), lambda j, i: (i, j)),
        compiler_params=_params("parallel", "parallel"),
        name="proj_gates",
    )(h, w)


def _proj_logf_kernel(h_ref, w_ref, b_ref, o_ref):
    o_ref[...] = _log_sigmoid(jnp.dot(h_ref[...], w_ref[...], preferred_element_type=F32) + b_ref[...])


def proj_logf(h, w, b):
    m, d = h.shape
    n = w.shape[1]
    return pl.pallas_call(
        _proj_logf_kernel,
        out_shape=jax.ShapeDtypeStruct((m, n), F32),
        grid=(m // ROW_TILE,),
        in_specs=[pl.BlockSpec((ROW_TILE, d), lambda i: (i, 0)), pl.BlockSpec((d, n), lambda i: (0, 0)),
                  pl.BlockSpec((1, n), lambda i: (0, 0))],
        out_specs=pl.BlockSpec((ROW_TILE, n), lambda i: (i, 0)),
        compiler_params=_params("parallel"),
        name="proj_logf",
    )(h, w, b)


def _cumsum_kernel(lf_ref, col_ref, row_ref, *, n_heads):
    x = lf_ref[...]
    t = x.shape[0]
    row = lax.broadcasted_iota(jnp.int32, x.shape, 0)
    sh = 1
    while sh < t:
        x = x + jnp.where(row >= sh, pltpu.roll(x, sh, 0), 0.0)
        sh *= 2
    col_ref[...] = x
    row_ref[...] = x.T[:n_heads, :]


def cumsum_logf(lf, batch, seq, n_heads):
    lanes = lf.shape[1]
    return pl.pallas_call(
        functools.partial(_cumsum_kernel, n_heads=n_heads),
        out_shape=(jax.ShapeDtypeStruct((batch * seq, lanes), F32), jax.ShapeDtypeStruct((batch, n_heads, seq), F32)),
        grid=(batch,),
        in_specs=[pl.BlockSpec((seq, lanes), lambda b: (b, 0))],
        out_specs=(pl.BlockSpec((seq, lanes), lambda b: (b, 0)), pl.BlockSpec((None, n_heads, seq), lambda b: (b, 0, 0))),
        compiler_params=_params("parallel"),
        name="cumsum_logf",
    )(lf)


def _softmax_pv(s, v):
    m = jnp.max(s, axis=1, keepdims=True)
    p = jnp.exp(s - m)
    l = jnp.sum(p, axis=1, keepdims=True)
    return jnp.dot(p.astype(BF16), v, preferred_element_type=F32) / l


def _causal_block(s_own):
    row = lax.broadcasted_iota(jnp.int32, s_own.shape, 0)
    col = lax.broadcasted_iota(jnp.int32, s_own.shape, 1)
    return jnp.where(col <= row, s_own, NEG)


def _moba_prompt_kernel(q_ref, k_ref, v_ref, kf_ref, o_ref, *, n_blocks):
    qi = pl.program_id(2)
    q = q_ref[...]

    def attend(n_past):
        keys = (n_past + 1) * MOBA_BLOCK
        s = lax.dot_general(q, k_ref[:keys, :], NT, preferred_element_type=F32) * ATTN_SCALE
        pieces = []
        if n_past > 0:
            means = jnp.mean(kf_ref[...].reshape(n_blocks, MOBA_BLOCK, HEAD_DIM), axis=1)
            gate = lax.dot_general(q, means.astype(BF16), NT, preferred_element_type=F32)
            n_iota = lax.broadcasted_iota(jnp.int32, gate.shape, 1)
            rank = jnp.zeros(gate.shape, jnp.int32)
            for m in range(n_past):
                gm = gate[:, m:m + 1]
                rank = rank + jnp.where(gm > gate, 1, jnp.where(gm == gate, jnp.where(m < n_iota, 1, 0), 0))
            chosen = jnp.where(rank < MOBA_TOPK, 1.0, 0.0)
            for n in range(n_past):
                pieces.append(jnp.where(chosen[:, n:n + 1] > 0.5, s[:, n * MOBA_BLOCK:(n + 1) * MOBA_BLOCK], NEG))
        pieces.append(_causal_block(s[:, n_past * MOBA_BLOCK:]))
        o_ref[...] = _softmax_pv(jnp.concatenate(pieces, axis=1), v_ref[:keys, :]).astype(o_ref.dtype)

    for n_past in range(n_blocks):
        pl.when(qi == n_past)(functools.partial(attend, n_past))


def moba_prompt(q, k, v, k_f32, batch, seq, n_heads):
    m, w = q.shape
    nq = seq // MOBA_BLOCK
    kv_spec = pl.BlockSpec((seq, HEAD_DIM), lambda b, h, qi: (b, h))
    q_spec = pl.BlockSpec((MOBA_BLOCK, HEAD_DIM), lambda b, h, qi: (b * nq + qi, h))
    return pl.pallas_call(
        functools.partial(_moba_prompt_kernel, n_blocks=nq),
        out_shape=jax.ShapeDtypeStruct((m, w), BF16),
        grid=(batch, n_heads, nq),
        in_specs=[q_spec, kv_spec, kv_spec, kv_spec],
        out_specs=q_spec,
        compiler_params=_params("parallel", "parallel", "parallel"),
        name="moba_prompt",
    )(q, k, v, k_f32)


def _fox_prompt_kernel(q_ref, k_ref, v_ref, ccol_ref, crow_ref, o_ref):
    h = pl.program_id(1)
    qi = pl.program_id(2)
    q = q_ref[...]
    tq = q.shape[0]
    ccol = ccol_ref[...]
    lane = lax.broadcasted_iota(jnp.int32, ccol.shape, 1)
    cq = jnp.sum(jnp.where(lane == h, ccol, 0.0), axis=1, keepdims=True)

    def attend(n_past):
        past = n_past * tq
        s = lax.dot_general(q, k_ref[:past + tq, :], NT, preferred_element_type=F32) * ATTN_SCALE
        s = s + (cq - crow_ref[pl.ds(h, 1), :past + tq])
        pieces = [s[:, :past]] if n_past > 0 else []
        pieces.append(_causal_block(s[:, past:]))
        o_ref[...] = _softmax_pv(jnp.concatenate(pieces, axis=1), v_ref[:past + tq, :]).astype(o_ref.dtype)

    for n_past in range(k_ref.shape[0] // tq):
        pl.when(qi == n_past)(functools.partial(attend, n_past))


def fox_prompt(q, k, v, cum_col, cum_row, batch, seq, n_heads, tq):
    m, w = q.shape
    nq = seq // tq
    kv_spec = pl.BlockSpec((seq, HEAD_DIM), lambda b, h, qi: (b, h))
    q_spec = pl.BlockSpec((tq, HEAD_DIM), lambda b, h, qi: (b * nq + qi, h))
    return pl.pallas_call(
        _fox_prompt_kernel,
        out_shape=jax.ShapeDtypeStruct((m, w), BF16),
        grid=(batch, n_heads, nq),
        in_specs=[
            q_spec, kv_spec, kv_spec,
            pl.BlockSpec((tq, cum_col.shape[1]), lambda b, h, qi: (b * nq + qi, 0)),
            pl.BlockSpec((None, n_heads, seq), lambda b, h, qi: (b, 0, 0)),
        ],
        out_specs=q_spec,
        compiler_params=_params("parallel", "parallel", "parallel"),
        name="fox_prompt",
    )(q, k, v, cum_col, cum_row)


def _own_head(shape, n_heads):
    row = lax.broadcasted_iota(jnp.int32, shape, 0)
    col = lax.broadcasted_iota(jnp.int32, shape, 1)
    return (col & (n_heads - 1)) == row, col


def _page_rows(ref):
    page, n_heads, hd = ref.shape
    return ref[...].reshape(page * n_heads, hd).astype(BF16)


def _page_spec(cache, index_map):
    page, n_heads, hd = cache.shape[2:]
    return pl.BlockSpec((None, None, page, n_heads, hd), index_map)


def _moba_gate_kernel(pt_ref, q_ref, *refs, pages, n_blocks, n_heads):
    del pt_ref
    k_refs = refs[:pages]
    o_ref = refs[pages]
    means_ref = refs[pages + 1]
    j = pl.program_id(1)
    per_block = MOBA_BLOCK // k_refs[0].shape[0]
    blocks_per_step = pages // per_block
    for r in range(blocks_per_step):
        tot = jnp.sum(k_refs[r * per_block][...], axis=0)
        for t in range(1, per_block):
            tot = tot + jnp.sum(k_refs[r * per_block + t][...], axis=0)
        means_ref[j * blocks_per_step + r] = tot * (1.0 / MOBA_BLOCK)

    @pl.when(j == pl.num_programs(1) - 1)
    def _():
        means = means_ref[...].reshape(n_blocks * n_heads, HEAD_DIM).astype(BF16)
        gate = lax.dot_general(q_ref[...], means, NT, preferred_element_type=F32)
        own, col = _own_head(gate.shape, n_heads)
        gate = jnp.where(own, gate, -jnp.inf)
        blk = lax.shift_right_logical(col, int(math.log2(n_heads))).astype(F32)
        lane = lax.broadcasted_iota(jnp.int32, o_ref.shape, 1)
        out = jnp.full(o_ref.shape, -1.0, F32)
        for r in range(MOBA_TOPK):
            mx = jnp.max(gate, axis=1, keepdims=True)
            ix = jnp.min(jnp.where(gate == mx, blk, float(n_blocks)), axis=1, keepdims=True)
            keep = jnp.where(jnp.abs(mx) < jnp.inf, ix, -1.0)
            out = jnp.where(lane == r, keep, out)
            gate = jnp.where(blk == ix, -jnp.inf, gate)
        o_ref[...] = out.astype(jnp.int32)


def moba_decode_select(page_table, q_heads, cache_k, layer):
    bsz, n_pages = page_table.shape
    page, n_heads, hd = cache_k.shape[2:]
    pages = DECODE_PAGES_PER_STEP
    n_blocks = n_pages * page // MOBA_BLOCK

    def k_map(r):
        return lambda b, j, pt: (pt[b, j * pages + r], layer, 0, 0, 0)

    grid_spec = pltpu.PrefetchScalarGridSpec(
        num_scalar_prefetch=1,
        grid=(bsz, n_pages // pages),
        in_specs=[pl.BlockSpec((None, n_heads, hd), lambda b, j, pt: (b, 0, 0))]
        + [_page_spec(cache_k, k_map(r)) for r in range(pages)],
        out_specs=pl.BlockSpec((None, n_heads, 128), lambda b, j, pt: (b, 0, 0)),
        scratch_shapes=[pltpu.VMEM((n_blocks, n_heads, hd), F32)],
    )
    return pl.pallas_call(
        functools.partial(_moba_gate_kernel, pages=pages, n_blocks=n_blocks, n_heads=n_heads),
        out_shape=jax.ShapeDtypeStruct((bsz, n_heads, 128), jnp.int32),
        grid_spec=grid_spec,
        compiler_params=_params("parallel", "arbitrary"),
        name="moba_decode_select",
    )(page_table, q_heads, *([cache_k] * pages))


def _moba_decode_attn_kernel(pt_ref, sel_ref, q_ref, kn_ref, vn_ref, ck_ref, cv_ref, o_ref, kbuf, vbuf, sem,
                             *, n_sel, per_block, n_heads, n_pages, layer):
    b = pl.program_id(0)
    n = n_sel * per_block
    page = kbuf.shape[3]
    slot = lax.rem(b, 2)

    def copies(req, buf_slot):
        out = []
        for h in range(n_heads):
            for r in range(n_sel):
                blk = jnp.maximum(sel_ref[(req * n_heads + h) * n_sel + r], 0)
                for t in range(per_block):
                    pg = pt_ref[req * n_pages + blk * per_block + t]
                    i = r * per_block + t
                    out.append(pltpu.make_async_copy(ck_ref.at[pg, layer, :, h, :], kbuf.at[buf_slot, h, i],
                                                     sem.at[buf_slot]))
                    out.append(pltpu.make_async_copy(cv_ref.at[pg, layer, :, h, :], vbuf.at[buf_slot, h, i],
                                                     sem.at[buf_slot]))
        return out

    @pl.when(b == 0)
    def _():
        for cp in copies(0, 0):
            cp.start()

    @pl.when(b + 1 < pl.num_programs(0))
    def _():
        for cp in copies(b + 1, 1 - slot):
            cp.start()

    for cp in copies(b, slot):
        cp.wait()

    q = q_ref[...].astype(F32)
    kn = kn_ref[...].astype(F32)
    vn = vn_ref[...].astype(F32)
    outs = []
    for h in range(n_heads):
        qh = q[h:h + 1, :]
        q8 = jnp.broadcast_to(qh, (8, HEAD_DIM)).astype(BF16)
        s_self = jnp.sum(qh * kn[h:h + 1, :], axis=1, keepdims=True) * ATTN_SCALE
        keys = kbuf[slot, h].reshape(n * page, HEAD_DIM).astype(BF16)
        s = lax.dot_general(q8, keys, NT, preferred_element_type=F32) * ATTN_SCALE
        pieces = []
        for r in range(n_sel):
            valid = sel_ref[(b * n_heads + h) * n_sel + r] >= 0
            pieces.append(jnp.where(valid, s[:, r * MOBA_BLOCK:(r + 1) * MOBA_BLOCK], NEG))
        s = jnp.concatenate(pieces, axis=1)
        m = jnp.maximum(jnp.max(s, axis=1, keepdims=True), s_self)
        p = jnp.exp(s - m)
        p_self = jnp.exp(s_self - m)
        l = jnp.sum(p, axis=1, keepdims=True) + p_self
        vals = vbuf[slot, h].reshape(n * page, HEAD_DIM).astype(BF16)
        acc = jnp.dot(p.astype(BF16), vals, preferred_element_type=F32)
        acc = acc + p_self.astype(BF16).astype(F32) * vn[h:h + 1, :]
        outs.append((acc / l)[0:1, :])
    o_ref[...] = jnp.concatenate(outs, axis=1).astype(o_ref.dtype)


def moba_decode_attend(page_table, sel, q_heads, k_new, v_new, cache_k, cache_v, layer):
    bsz, n_pages = page_table.shape
    page, n_heads, hd = cache_k.shape[2:]
    per_block = MOBA_BLOCK // page
    n_sel = sel.shape[-1]
    n = n_sel * per_block
    heads = pl.BlockSpec((None, n_heads, hd), lambda b, pt, sl: (b, 0, 0))
    grid_spec = pltpu.PrefetchScalarGridSpec(
        num_scalar_prefetch=2,
        grid=(bsz,),
        in_specs=[heads, heads, heads, pl.BlockSpec(memory_space=pl.ANY), pl.BlockSpec(memory_space=pl.ANY)],
        out_specs=pl.BlockSpec((None, 1, n_heads * hd), lambda b, pt, sl: (b, 0, 0)),
        scratch_shapes=[pltpu.VMEM((2, n_heads, n, page, hd), cache_k.dtype),
                        pltpu.VMEM((2, n_heads, n, page, hd), cache_v.dtype),
                        pltpu.SemaphoreType.DMA((2,))],
    )
    out = pl.pallas_call(
        functools.partial(_moba_decode_attn_kernel, n_sel=n_sel, per_block=per_block, n_heads=n_heads,
                          n_pages=n_pages, layer=layer),
        out_shape=jax.ShapeDtypeStruct((bsz, 1, n_heads * hd), BF16),
        grid_spec=grid_spec,
        compiler_params=_params("arbitrary"),
        name="moba_decode_attend",
    )(page_table.reshape(-1), sel.reshape(-1), q_heads, k_new, v_new, cache_k, cache_v)
    return out.reshape(bsz, n_heads * hd)


def _fox_decode_kernel(pt_ref, q_ref, kn_ref, vn_ref, lfn_ref, *refs, pages, n_heads):
    del pt_ref
    k_refs = refs[:pages]
    v_refs = refs[pages:2 * pages]
    lf_refs = refs[2 * pages:3 * pages]
    o_ref = refs[3 * pages]
    m_ref, l_ref, carry_ref, acc_ref = refs[3 * pages + 1:]
    j = pl.program_id(1)
    q = q_ref[...]
    width = carry_ref.shape[1]
    own, col = _own_head((n_heads, width), n_heads)
    col1 = col[0:1, :]

    @pl.when(j == 0)
    def _():
        s_self = jnp.sum(q.astype(F32) * kn_ref[...].astype(F32), axis=1, keepdims=True) * ATTN_SCALE
        m_ref[...] = jnp.broadcast_to(s_self, m_ref.shape)
        l_ref[...] = jnp.ones(l_ref.shape, F32)
        acc_ref[...] = vn_ref[...].astype(F32)
        lanes = lfn_ref.shape[1]
        t = jnp.where(lax.broadcasted_iota(jnp.int32, (1, lanes), 1) < n_heads, lfn_ref[...], 0.0)
        sh = n_heads
        while sh < lanes:
            t = t + pltpu.roll(t, sh, 1)
            sh *= 2
        carry_ref[...] = jnp.concatenate([t] * (width // lanes), axis=1)

    carry = carry_ref[...]
    scores = []
    for r in range(pages):
        lf = lf_refs[r][...]
        suf = lf
        tot = lf
        sh = n_heads
        while sh < width:
            suf = suf + jnp.where(col1 + sh < width, pltpu.roll(suf, width - sh, 1), 0.0)
            tot = tot + pltpu.roll(tot, sh, 1)
            sh *= 2
        bias = carry + (suf - lf)
        carry = carry + tot
        s = lax.dot_general(q, _page_rows(k_refs[r]), NT, preferred_element_type=F32) * ATTN_SCALE + bias
        scores.append(jnp.where(own, s, NEG))
    carry_ref[...] = carry

    m_old = m_ref[...]
    mx = jnp.max(scores[0], axis=1, keepdims=True)
    for s in scores[1:]:
        mx = jnp.maximum(mx, jnp.max(s, axis=1, keepdims=True))
    m_new = jnp.maximum(m_old, mx)
    alpha = jnp.exp(m_old - m_new)
    l_new = alpha * l_ref[...]
    acc = alpha[:, 0:1] * acc_ref[...]
    for r in range(pages):
        p = jnp.exp(scores[r] - m_new[:, 0:1])
        l_new = l_new + jnp.sum(p, axis=1, keepdims=True)
        acc = acc + jnp.dot(p.astype(BF16), _page_rows(v_refs[r]), preferred_element_type=F32)
    m_ref[...] = m_new
    l_ref[...] = l_new
    acc_ref[...] = acc

    @pl.when(j == pl.num_programs(1) - 1)
    def _():
        o_ref[...] = (acc_ref[...] / l_ref[:, 0:1]).astype(o_ref.dtype)


def fox_decode(page_table, q_heads, k_new, v_new, lf_new, cache_k, cache_v, cache_lf, layer):
    bsz, n_pages = page_table.shape
    page, n_heads, hd = cache_k.shape[2:]
    pages = DECODE_PAGES_PER_STEP
    width = page * n_heads

    def page_map(r):
        return lambda b, j, pt: (pt[b, n_pages - 1 - (j * pages + r)], layer, 0, 0, 0)

    def lf_map(r):
        return lambda b, j, pt: (pt[b, n_pages - 1 - (j * pages + r)], layer, 0, 0)

    heads = pl.BlockSpec((None, n_heads, hd), lambda b, j, pt: (b, 0, 0))
    grid_spec = pltpu.PrefetchScalarGridSpec(
        num_scalar_prefetch=1,
        grid=(bsz, n_pages // pages),
        in_specs=[heads, heads, heads, pl.BlockSpec((None, 1, lf_new.shape[-1]), lambda b, j, pt: (b, 0, 0))]
        + [_page_spec(cache_k, page_map(r)) for r in range(pages)]
        + [_page_spec(cache_v, page_map(r)) for r in range(pages)]
        + [pl.BlockSpec((None, None, 1, width), lf_map(r)) for r in range(pages)],
        out_specs=heads,
        scratch_shapes=[pltpu.VMEM((n_heads, 128), F32), pltpu.VMEM((n_heads, 128), F32),
                        pltpu.VMEM((1, width), F32), pltpu.VMEM((n_heads, hd), F32)],
    )
    out = pl.pallas_call(
        functools.partial(_fox_decode_kernel, pages=pages, n_heads=n_heads),
        out_shape=jax.ShapeDtypeStruct((bsz, n_heads, hd), BF16),
        grid_spec=grid_spec,
        compiler_params=_params("parallel", "arbitrary"),
        name="fox_decode",
    )(page_table, q_heads, k_new, v_new, lf_new,
      *([cache_k] * pages), *([cache_v] * pages), *([cache_lf] * pages))
    return out.reshape(bsz, n_heads * hd)


def _merge_kernel(oa_ref, of_ref, ga_ref, gb_ref, wa_ref, wb_ref, wo_ref, x_ref, g_ref, x1_ref, xn_ref):
    ya = jnp.dot(oa_ref[...], wa_ref[...], preferred_element_type=F32)
    yf = jnp.dot(of_ref[...], wb_ref[...], preferred_element_type=F32)
    mix = ga_ref[...] * ya + gb_ref[...] * yf
    x1 = x_ref[...] + jnp.dot(mix.astype(BF16), wo_ref[...], preferred_element_type=F32)
    x1_ref[...] = x1
    xn_ref[...] = _rms_scale(x1, g_ref[...]).astype(BF16)


def merge_out(oa, of, gates, wa, wb, wo, x, g_next):
    m, d = x.shape
    wdt = oa.shape[1]
    row = lambda width: pl.BlockSpec((ROW_TILE, width), lambda i: (i, 0))
    return pl.pallas_call(
        _merge_kernel,
        out_shape=(jax.ShapeDtypeStruct((m, d), F32), jax.ShapeDtypeStruct((m, d), BF16)),
        grid=(m // ROW_TILE,),
        in_specs=[row(wdt), row(wdt), row(d), pl.BlockSpec((ROW_TILE, d), lambda i: (i, 1)),
                  _resident(wa.shape, lambda i: (0, 0)), _resident(wb.shape, lambda i: (0, 0)),
                  _resident(wo.shape, lambda i: (0, 0)), row(d), _resident((1, d), lambda i: (0, 0))],
        out_specs=(row(d), row(d)),
        compiler_params=_params("parallel"),
        name="merge_out",
    )(oa, of, gates, gates, wa, wb, wo, x, g_next.reshape(1, d))


def _matmul_bf16_kernel(a_ref, w_ref, o_ref):
    o_ref[...] = jnp.dot(a_ref[...], w_ref[...], preferred_element_type=F32).astype(o_ref.dtype)


def matmul_bf16(a, w, tn):
    m, d = a.shape
    n = w.shape[1]
    return pl.pallas_call(
        _matmul_bf16_kernel,
        out_shape=jax.ShapeDtypeStruct((m, n), BF16),
        grid=(n // tn, m // ROW_TILE),
        in_specs=[pl.BlockSpec((ROW_TILE, d), lambda j, i: (i, 0)), pl.BlockSpec((d, tn), lambda j, i: (0, j))],
        out_specs=pl.BlockSpec((ROW_TILE, tn), lambda j, i: (i, j)),
        compiler_params=_params("parallel", "parallel"),
        name="peer_query",
    )(a, w)


def _top_values(x, k):
    vals = []
    for _ in range(k):
        mx = jnp.max(x, axis=0, keepdims=True)
        vals.append(mx)
        x = jnp.where(x == mx, NEG, x)
    return vals


def _peer_route_kernel(q_ref, k1_ref, k2_ref, s1_ref, s2_ref, e1_ref, e2_ref, thr_ref, *, n_heads):
    dk = k1_ref.shape[1]
    k1 = k1_ref[...]
    k2 = k2_ref[...]
    for h in range(n_heads):
        q1 = q_ref[:, (2 * h) * dk:(2 * h + 1) * dk]
        q2 = q_ref[:, (2 * h + 1) * dk:(2 * h + 2) * dk]
        s1 = lax.dot_general(k1, q1, NT, preferred_element_type=F32)
        s2 = lax.dot_general(k2, q2, NT, preferred_element_type=F32)
        t1 = _top_values(s1, PEER_TOPK)
        t2 = _top_values(s2, PEER_TOPK)
        half = PEER_TOPK // 2
        t2_all = jnp.concatenate(t2, axis=0)
        t2_low = jnp.concatenate(t2[:half], axis=0)
        b_rank = lax.broadcasted_iota(jnp.int32, t2_low.shape, 0)
        cand = [t1[0] + t2_all]
        for a in range(1, half):
            cand.append(jnp.where(b_rank < PEER_TOPK // (a + 1), t1[a] + t2_low, NEG))
        cand.append(jnp.concatenate(t1[half:], axis=0) + t2[0])
        top = _top_values(jnp.concatenate(cand, axis=0), PEER_TOPK)
        z = jnp.zeros_like(top[0])
        for val in top:
            z = z + jnp.exp(val - top[0])
        s1_ref[h] = s1
        s2_ref[h] = s2
        e1_ref[h] = jnp.exp(s1 - t1[0])
        e2_ref[h] = jnp.exp(s2 - t2[0]) / z
        thr_ref[pl.ds(h, 1), :] = top[PEER_TOPK - 1]


def peer_route(qp, k1, k2, n_heads):
    m, w = qp.shape
    n_keys = k1.shape[0]
    big = jax.ShapeDtypeStruct((n_heads, n_keys, m), F32)
    big_spec = pl.BlockSpec((n_heads, n_keys, ROW_TILE), lambda i: (0, 0, i))
    return pl.pallas_call(
        functools.partial(_peer_route_kernel, n_heads=n_heads),
        out_shape=(big, big, big, big, jax.ShapeDtypeStruct((n_heads, m), F32)),
        grid=(m // ROW_TILE,),
        in_specs=[pl.BlockSpec((ROW_TILE, w), lambda i: (i, 0)), pl.BlockSpec(k1.shape, lambda i: (0, 0)),
                  pl.BlockSpec(k2.shape, lambda i: (0, 0))],
        out_specs=(big_spec, big_spec, big_spec, big_spec, pl.BlockSpec((n_heads, ROW_TILE), lambda i: (0, i))),
        compiler_params=_params("parallel"),
        name="peer_route",
    )(qp, k1, k2)


def _peer_dense_kernel(xn_ref, u_ref, v_ref, s1p_ref, s1c_ref, s2_ref, e1p_ref, e1c_ref, e2_ref, thr_ref, x_ref,
                       o_ref, at0_ref, at1_ref, cg0_ref, cg1_ref, *, n_heads, n_keys):
    k = pl.program_id(1)
    th, tm = at0_ref.shape
    groups = th // n_keys
    chunk = 32
    chunks = n_keys // chunk
    lane_tile = 128

    @pl.when(k == 0)
    def _():
        o_ref[...] = x_ref[...]
        at1_ref[...] = jnp.zeros(at1_ref.shape, F32)
        cg0_ref[...] = jnp.zeros(cg0_ref.shape, BF16)

    def activations(half, at_ref):
        at_ref[...] = lax.dot_general(u_ref[half * th:(half + 1) * th, :], xn_ref[...], NT,
                                      preferred_element_type=F32)

    def coefficients(s1_ref, e1_ref, half, at_ref, cg_ref):
        for lt in range(tm // lane_tile):
            lanes = slice(lt * lane_tile, (lt + 1) * lane_tile)
            for c in range(chunks):
                rows = slice(c * chunk, (c + 1) * chunk)
                accs = [jnp.zeros((chunk, lane_tile), F32) for _ in range(groups)]
                for h in range(n_heads):
                    s2c = s2_ref[h, rows, lanes]
                    e2c = e2_ref[h, rows, lanes]
                    thr = thr_ref[h:h + 1, lanes]
                    for ig in range(groups):
                        r = half * groups + ig
                        pair = s2c + s1_ref[h, r:r + 1, lanes]
                        wgt = e2c * e1_ref[h, r:r + 1, lanes]
                        accs[ig] = jnp.where(pair >= thr, accs[ig] + wgt, accs[ig])
                for ig in range(groups):
                    a_rows = slice(ig * n_keys + c * chunk, ig * n_keys + (c + 1) * chunk)
                    a = at_ref[a_rows, lanes]
                    gelu = 0.5 * a * (1.0 + lax.erf(a * SQRT_HALF))
                    cg_ref[a_rows, lanes] = (accs[ig] * gelu).astype(BF16)

    def accumulate(half, cg_ref):
        o_ref[...] += lax.dot_general(cg_ref[...], v_ref[half * th:(half + 1) * th, :], TN,
                                      preferred_element_type=F32)

    activations(0, at0_ref)
    coefficients(s1p_ref, e1p_ref, 1, at1_ref, cg1_ref)
    accumulate(0, cg0_ref)
    activations(1, at1_ref)
    coefficients(s1c_ref, e1c_ref, 0, at0_ref, cg0_ref)
    accumulate(1, cg1_ref)


def peer_dense(xn, u, v, s1, s2, e1, e2, thr, x):
    m, d = xn.shape
    n_exp = u.shape[0]
    n_heads, n_keys = s1.shape[:2]
    te = PEER_EXPERT_TILE
    th = te // 2
    n_tiles = n_exp // te
    per_tile = te // n_keys
    cur = lambda i, k: (0, jnp.minimum(k, n_tiles - 1), i)
    prev = lambda i, k: (0, jnp.maximum(k - 1, 0), i)
    rows_spec = lambda index_map: pl.BlockSpec((n_heads, per_tile, ROW_TILE), index_map)
    big_spec = pl.BlockSpec((n_heads, n_keys, ROW_TILE), lambda i, k: (0, 0, i))
    return pl.pallas_call(
        functools.partial(_peer_dense_kernel, n_heads=n_heads, n_keys=n_keys),
        out_shape=jax.ShapeDtypeStruct((m, d), F32),
        grid=(m // ROW_TILE, n_tiles + 1),
        in_specs=[pl.BlockSpec((ROW_TILE, d), lambda i, k: (i, 0)),
                  pl.BlockSpec((te, d), lambda i, k: (jnp.minimum(k, n_tiles - 1), 0)),
                  pl.BlockSpec((te, d), lambda i, k: (jnp.maximum(k - 1, 0), 0)),
                  rows_spec(prev), rows_spec(cur), big_spec, rows_spec(prev), rows_spec(cur), big_spec,
                  pl.BlockSpec((n_heads, ROW_TILE), lambda i, k: (0, i)),
                  pl.BlockSpec((ROW_TILE, d), lambda i, k: (i, 0))],
        out_specs=pl.BlockSpec((ROW_TILE, d), lambda i, k: (i, 0)),
        scratch_shapes=[pltpu.VMEM((th, ROW_TILE), F32), pltpu.VMEM((th, ROW_TILE), F32),
                        pltpu.VMEM((th, ROW_TILE), BF16), pltpu.VMEM((th, ROW_TILE), BF16)],
        compiler_params=_params("parallel", "arbitrary"),
        name="peer_dense",
    )(xn, u, v, s1, s1, s2, e1, e1, e2, thr, x)


def _ple_kernel(x_ref, p_ref, g_ref, wg_ref, wp_ref, gf_ref, y_ref):
    x = x_ref[...]
    gate = _sigmoid(jnp.dot(_rms_scale(x, g_ref[...]).astype(BF16), wg_ref[...], preferred_element_type=F32))
    emb = jnp.dot(p_ref[...].astype(BF16), wp_ref[...], preferred_element_type=F32)
    y_ref[...] = _rms_scale(x + gate * emb, gf_ref[...])


def ple_final(x, p, g_ple, wg, wp, g_final):
    m, d = x.shape
    row = lambda width: pl.BlockSpec((ROW_TILE, width), lambda i: (i, 0))
    vec = _resident((1, d), lambda i: (0, 0))
    return pl.pallas_call(
        _ple_kernel,
        out_shape=jax.ShapeDtypeStruct((m, d), F32),
        grid=(m // ROW_TILE,),
        in_specs=[row(d), row(p.shape[1]), vec, _resident(wg.shape, lambda i: (0, 0)),
                  _resident(wp.shape, lambda i: (0, 0)), vec],
        out_specs=row(d),
        compiler_params=_params("parallel"),
        name="ple_final",
    )(x, p, g_ple.reshape(1, d), wg, wp, g_final.reshape(1, d))


def _rope_tables(pos):
    half = HEAD_DIM // 2
    inv_freq = ROPE_THETA ** (-jnp.arange(half, dtype=F32) / half)
    ang = pos.astype(F32)[:, None] * inv_freq[None, :]
    cos, sin = jnp.cos(ang), jnp.sin(ang)
    return jnp.concatenate([cos, cos], axis=1), jnp.concatenate([-sin, sin], axis=1)


def _pad_rows(a, rows):
    return jnp.pad(a, ((0, rows - a.shape[0]),) + ((0, 0),) * (a.ndim - 1))


def _token_stage_in(x, pos, g_attn, w_qkv, w_fl, b_fl, w_gates, seg):
    h = rms_cast(x, g_attn)
    cos, sin = _rope_tables(pos)
    zf, zb = [], []
    for j in range(w_qkv.shape[1] // seg):
        is_cache_out = j % 3 != 0
        outs = proj_heads(h, w_qkv[:, j * seg:(j + 1) * seg], cos, sin, rope=j < 2, keep_f32=is_cache_out)
        zb.append(outs[0])
        zf.append(outs[1] if is_cache_out else None)
    gates = proj_sigmoid(h, w_gates, tn=1024)
    lf = proj_logf(h, w_fl, b_fl)
    return zf, zb, gates, lf


def _token_stage_out(x, oa, of, gates, p, wts):
    x1, xn = merge_out(oa, of, gates, wts["wa"], wts["wb"], wts["wo"], x, wts["g_ffn"])
    qp = matmul_bf16(xn, wts["wq"], tn=1024)
    s1, s2, e1, e2, thr = peer_route(qp, wts["k1"], wts["k2"], wts["peer_heads"])
    x2 = peer_dense(xn, wts["u"], wts["v"], s1, s2, e1, e2, thr, x1)
    return ple_final(x2, p, wts["g_ple"], wts["wg"], wts["wp"], wts["g_final"])


def kernel(x_prompt, x_sample, cache_moba_k, cache_moba_v, cache_fox_k, cache_fox_v, cache_fox_logf, page_table, p_prompt, p_sample, g_attn, w_in, b_forget, w_branch_a, w_branch_b, w_out, g_ffn, w_peer_q, peer_subkey_1, peer_subkey_2, peer_u, peer_v, g_ple, w_ple_gate, w_ple_proj, g_final):
    batch, seq, d = x_prompt.shape
    dec_batch, dec_seq, _ = x_sample.shape
    n_pool, depth, page, n_moba, hd = cache_moba_k.shape
    n_fox = cache_fox_k.shape[3]
    n_pages = page_table.shape[1]
    past_len = n_pages * page
    wa_w, wb_w = n_moba * hd, n_fox * hd
    assert depth == 1 and dec_seq == 1 and hd == HEAD_DIM and wa_w == wb_w
    assert n_moba & (n_moba - 1) == 0 and n_fox & (n_fox - 1) == 0
    assert seq % MOBA_BLOCK == 0 and past_len % MOBA_BLOCK == 0 and MOBA_BLOCK % page == 0
    layer = 0
    n_qkv = 3 * wa_w + 3 * wb_w
    peer_heads = w_peer_q.shape[2] // (2 * peer_subkey_1.shape[2])
    lanes = 128

    w_l = w_in[layer]
    w_qkv = w_l[:, :n_qkv].astype(BF16)
    w_fl = jnp.pad(w_l[:, n_qkv:n_qkv + n_fox], ((0, 0), (0, lanes - n_fox))).astype(BF16)
    b_fl = jnp.pad(b_forget[layer], (0, lanes - n_fox)).reshape(1, lanes)
    w_gates = w_l[:, n_qkv + n_fox:].astype(BF16)
    wts = dict(
        wa=w_branch_a[layer].astype(BF16), wb=w_branch_b[layer].astype(BF16), wo=w_out[layer].astype(BF16),
        g_ffn=g_ffn[layer], wq=w_peer_q[layer].astype(BF16),
        k1=peer_subkey_1[layer].astype(BF16), k2=peer_subkey_2[layer].astype(BF16), peer_heads=peer_heads,
        u=peer_u[layer].astype(BF16), v=peer_v[layer].astype(BF16),
        g_ple=g_ple[layer], wg=w_ple_gate[layer].astype(BF16), wp=w_ple_proj[layer].astype(BF16), g_final=g_final,
    )

    m_p = batch * seq
    xp = x_prompt.reshape(m_p, d)
    pos_p = jnp.tile(jnp.arange(seq, dtype=jnp.int32), batch)
    zf_p, zb_p, gates_p, lf_p = _token_stage_in(xp, pos_p, g_attn[layer].reshape(1, d), w_qkv, w_fl, b_fl, w_gates, wa_w)
    cum_col, cum_row = cumsum_logf(lf_p, batch, seq, n_fox)
    oa_p = moba_prompt(zb_p[0], zb_p[1], zb_p[2], zf_p[1], batch, seq, n_moba)
    of_p = fox_prompt(zb_p[3], zb_p[4], zb_p[5], cum_col, cum_row, batch, seq, n_fox, tq=MOBA_BLOCK)
    y_p = _token_stage_out(xp, oa_p, of_p, gates_p, p_prompt[layer].reshape(m_p, -1), wts)

    m_s = ROW_TILE * (-(-dec_batch // ROW_TILE))
    xs = _pad_rows(x_sample.reshape(dec_batch, d), m_s)
    pos_s = jnp.full((m_s,), past_len, jnp.int32)
    zf_s, zb_s, gates_s, lf_s = _token_stage_in(xs, pos_s, g_attn[layer].reshape(1, d), w_qkv, w_fl, b_fl, w_gates, wa_w)
    per_head = lambda seg, n: zb_s[seg][:dec_batch].reshape(dec_batch, n, hd)
    qa_s = per_head(0, n_moba)
    sel = moba_decode_select(page_table, qa_s, cache_moba_k, layer)[:, :, :MOBA_TOPK]
    oa_s = moba_decode_attend(page_table, sel, qa_s, per_head(1, n_moba), per_head(2, n_moba),
                              cache_moba_k, cache_moba_v, layer)
    lf_pages = cache_fox_logf.reshape(n_pool, depth, 1, page * n_fox)
    of_s = fox_decode(page_table, per_head(3, n_fox), per_head(4, n_fox), per_head(5, n_fox),
                      lf_s[:dec_batch].reshape(dec_batch, 1, lanes), cache_fox_k, cache_fox_v, lf_pages, layer)
    y_s = _token_stage_out(xs, _pad_rows(oa_s, m_s), _pad_rows(of_s, m_s), gates_s,
                           _pad_rows(p_sample[layer].reshape(dec_batch, -1), m_s), wts)

    def kv_p(a, n):
        return a.reshape(batch, 1, seq, n, hd)

    def kv_s(a, n):
        return a[:dec_batch].reshape(dec_batch, 1, 1, n, hd)

    return (y_p.reshape(batch, seq, d), y_s[:dec_batch].reshape(dec_batch, 1, d),
            kv_p(zf_p[1], n_moba), kv_p(zf_p[2], n_moba), kv_p(zf_p[4], n_fox), kv_p(zf_p[5], n_fox),
            lf_p[:, :n_fox].reshape(batch, 1, seq, n_fox),
            kv_s(zf_s[1], n_moba), kv_s(zf_s[2], n_moba), kv_s(zf_s[4], n_fox), kv_s(zf_s[5], n_fox),
            lf_s[:dec_batch, :n_fox].reshape(dec_batch, 1, 1, n_fox))
```

```python
import functools
import math

import jax
import jax.numpy as jnp
from jax import lax
from jax.experimental import pallas as pl
from jax.experimental.pallas import tpu as pltpu

F32 = jnp.float32
BF16 = jnp.bfloat16

HEAD_DIM = 128
MOBA_BLOCK = 256
MOBA_TOPK = 3
ROPE_THETA = 10000.0
PEER_TOPK = 16
RMS_EPS = 1e-6
ATTN_SCALE = HEAD_DIM ** -0.5
NEG = -1e30
SQRT_HALF = 0.7071067811865476

V7X_VMEM_LIMIT = 56 * 1024 * 1024
ROW_TILE = 256
PEER_EXPERT_TILE = 1024
DECODE_PAGES_PER_STEP = 16

NT = (((1,), (1,)), ((), ()))
TN = (((0,), (0,)), ((), ()))


def _params(*sem):
    return pltpu.CompilerParams(dimension_semantics=sem, vmem_limit_bytes=V7X_VMEM_LIMIT)


def _resident(shape, index_map):
    return pl.BlockSpec(shape, index_map, pipeline_mode=pl.Buffered(1))


def _rms_scale(x, g):
    r = lax.rsqrt(jnp.mean(x * x, axis=-1, keepdims=True) + RMS_EPS)
    return x * r * g


def _log_sigmoid(x):
    return -(jnp.maximum(-x, 0.0) + jnp.log1p(jnp.exp(-jnp.abs(x))))


def _sigmoid(x):
    return 1.0 / (1.0 + jnp.exp(-x))


def _rms_cast_kernel(x_ref, g_ref, o_ref):
    o_ref[...] = _rms_scale(x_ref[...], g_ref[...]).astype(o_ref.dtype)


def rms_cast(x, g):
    m, d = x.shape
    return pl.pallas_call(
        _rms_cast_kernel,
        out_shape=jax.ShapeDtypeStruct((m, d), BF16),
        grid=(m // ROW_TILE,),
        in_specs=[pl.BlockSpec((ROW_TILE, d), lambda i: (i, 0)), pl.BlockSpec((1, d), lambda i: (0, 0))],
        out_specs=pl.BlockSpec((ROW_TILE, d), lambda i: (i, 0)),
        compiler_params=_params("parallel"),
        name="rms_cast",
    )(x, g.reshape(1, d))


def _proj_heads_kernel(h_ref, w_ref, *refs, rope, keep_f32):
    z = jnp.dot(h_ref[...], w_ref[...], preferred_element_type=F32)
    out_refs = refs[2:] if rope else refs
    if rope:
        cos = refs[0][...]
        sin = refs[1][...]
        parts = []
        for hh in range(z.shape[1] // HEAD_DIM):
            zh = z[:, hh * HEAD_DIM:(hh + 1) * HEAD_DIM]
            parts.append(zh * cos + pltpu.roll(zh, HEAD_DIM // 2, 1) * sin)
        z = jnp.concatenate(parts, axis=1)
    out_refs[0][...] = z.astype(BF16)
    if keep_f32:
        out_refs[1][...] = z


def proj_heads(h, w, cos, sin, rope, keep_f32):
    m, d = h.shape
    n = w.shape[1]
    row = lambda width: pl.BlockSpec((ROW_TILE, width), lambda i: (i, 0))
    out_shape = [jax.ShapeDtypeStruct((m, n), BF16)] + ([jax.ShapeDtypeStruct((m, n), F32)] if keep_f32 else [])
    return pl.pallas_call(
        functools.partial(_proj_heads_kernel, rope=rope, keep_f32=keep_f32),
        out_shape=out_shape,
        grid=(m // ROW_TILE,),
        in_specs=[row(d), _resident((d, n), lambda i: (0, 0))] + ([row(HEAD_DIM), row(HEAD_DIM)] if rope else []),
        out_specs=[row(n)] * len(out_shape),
        compiler_params=_params("parallel"),
        name="proj_heads",
    )(h, w, *((cos, sin) if rope else ()))


def _proj_sigmoid_kernel(h_ref, w_ref, o_ref):
    o_ref[...] = _sigmoid(jnp.dot(h_ref[...], w_ref[...], preferred_element_type=F32))


def proj_sigmoid(h, w, tn):
    m, d = h.shape
    n = w.shape[1]
    return pl.pallas_call(
        _proj_sigmoid_kernel,
        out_shape=jax.ShapeDtypeStruct((m, n), F32),
        grid=(n // tn, m // ROW_TILE),
        in_specs=[pl.BlockSpec((ROW_TILE, d), lambda j, i: (i, 0)), pl.BlockSpec((d, tn), lambda j, i: (0, j))],
        out_specs=pl.BlockSpec((ROW_TILE, tn), lambda j, i: (i, j)),
        compiler_params=_params("parallel", "parallel"),
        name="proj_gates",
    )(h, w)


def _proj_logf_kernel(h_ref, w_ref, b_ref, o_ref):
    o_ref[...] = _log_sigmoid(jnp.dot(h_ref[...], w_ref[...], preferred_element_type=F32) + b_ref[...])


def proj_logf(h, w, b):
    m, d = h.shape
    n = w.shape[1]
    return pl.pallas_call(
        _proj_logf_kernel,
        out_shape=jax.ShapeDtypeStruct((m, n), F32),
        grid=(m // ROW_TILE,),
        in_specs=[pl.BlockSpec((ROW_TILE, d), lambda i: (i, 0)), pl.BlockSpec((d, n), lambda i: (0, 0)),
                  pl.BlockSpec((1, n), lambda i: (0, 0))],
        out_specs=pl.BlockSpec((ROW_TILE, n), lambda i: (i, 0)),
        compiler_params=_params("parallel"),
        name="proj_logf",
    )(h, w, b)


def _cumsum_kernel(lf_ref, col_ref, row_ref, *, n_heads):
    x = lf_ref[...]
    t = x.shape[0]
    row = lax.broadcasted_iota(jnp.int32, x.shape, 0)
    sh = 1
    while sh < t:
        x = x + jnp.where(row >= sh, pltpu.roll(x, sh, 0), 0.0)
        sh *= 2
    col_ref[...] = x
    row_ref[...] = x.T[:n_heads, :]


def cumsum_logf(lf, batch, seq, n_heads):
    lanes = lf.shape[1]
    return pl.pallas_call(
        functools.partial(_cumsum_kernel, n_heads=n_heads),
        out_shape=(jax.ShapeDtypeStruct((batch * seq, lanes), F32), jax.ShapeDtypeStruct((batch, n_heads, seq), F32)),
        grid=(batch,),
        in_specs=[pl.BlockSpec((seq, lanes), lambda b: (b, 0))],
        out_specs=(pl.BlockSpec((seq, lanes), lambda b: (b, 0)), pl.BlockSpec((None, n_heads, seq), lambda b: (b, 0, 0))),
        compiler_params=_params("parallel"),
        name="cumsum_logf",
    )(lf)


def _softmax_pv(s, v):
    m = jnp.max(s, axis=1, keepdims=True)
    p = jnp.exp(s - m)
    l = jnp.sum(p, axis=1, keepdims=True)
    return jnp.dot(p.astype(BF16), v, preferred_element_type=F32) / l


def _causal_block(s_own):
    row = lax.broadcasted_iota(jnp.int32, s_own.shape, 0)
    col = lax.broadcasted_iota(jnp.int32, s_own.shape, 1)
    return jnp.where(col <= row, s_own, NEG)


def _moba_prompt_kernel(q_ref, k_ref, v_ref, kf_ref, o_ref, *, n_blocks):
    qi = pl.program_id(2)
    q = q_ref[...]

    def attend(n_past):
        keys = (n_past + 1) * MOBA_BLOCK
        s = lax.dot_general(q, k_ref[:keys, :], NT, preferred_element_type=F32) * ATTN_SCALE
        pieces = []
        if n_past > 0:
            means = jnp.mean(kf_ref[...].reshape(n_blocks, MOBA_BLOCK, HEAD_DIM), axis=1)
            gate = lax.dot_general(q, means.astype(BF16), NT, preferred_element_type=F32)
            n_iota = lax.broadcasted_iota(jnp.int32, gate.shape, 1)
            rank = jnp.zeros(gate.shape, jnp.int32)
            for m in range(n_past):
                gm = gate[:, m:m + 1]
                rank = rank + jnp.where(gm > gate, 1, jnp.where(gm == gate, jnp.where(m < n_iota, 1, 0), 0))
            chosen = jnp.where(rank < MOBA_TOPK, 1.0, 0.0)
            for n in range(n_past):
                pieces.append(jnp.where(chosen[:, n:n + 1] > 0.5, s[:, n * MOBA_BLOCK:(n + 1) * MOBA_BLOCK], NEG))
        pieces.append(_causal_block(s[:, n_past * MOBA_BLOCK:]))
        o_ref[...] = _softmax_pv(jnp.concatenate(pieces, axis=1), v_ref[:keys, :]).astype(o_ref.dtype)

    for n_past in range(n_blocks):
        pl.when(qi == n_past)(functools.partial(attend, n_past))


def moba_prompt(q, k, v, k_f32, batch, seq, n_heads):
    m, w = q.shape
    nq = seq // MOBA_BLOCK
    kv_spec = pl.BlockSpec((seq, HEAD_DIM), lambda b, h, qi: (b, h))
    q_spec = pl.BlockSpec((MOBA_BLOCK, HEAD_DIM), lambda b, h, qi: (b * nq + qi, h))
    return pl.pallas_call(
        functools.partial(_moba_prompt_kernel, n_blocks=nq),
        out_shape=jax.ShapeDtypeStruct((m, w), BF16),
        grid=(batch, n_heads, nq),
        in_specs=[q_spec, kv_spec, kv_spec, kv_spec],
        out_specs=q_spec,
        compiler_params=_params("parallel", "parallel", "parallel"),
        name="moba_prompt",
    )(q, k, v, k_f32)


def _fox_prompt_kernel(q_ref, k_ref, v_ref, ccol_ref, crow_ref, o_ref):
    h = pl.program_id(1)
    qi = pl.program_id(2)
    q = q_ref[...]
    tq = q.shape[0]
    ccol = ccol_ref[...]
    lane = lax.broadcasted_iota(jnp.int32, ccol.shape, 1)
    cq = jnp.sum(jnp.where(lane == h, ccol, 0.0), axis=1, keepdims=True)

    def attend(n_past):
        past = n_past * tq
        s = lax.dot_general(q, k_ref[:past + tq, :], NT, preferred_element_type=F32) * ATTN_SCALE
        s = s + (cq - crow_ref[pl.ds(h, 1), :past + tq])
        pieces = [s[:, :past]] if n_past > 0 else []
        pieces.append(_causal_block(s[:, past:]))
        o_ref[...] = _softmax_pv(jnp.concatenate(pieces, axis=1), v_ref[:past + tq, :]).astype(o_ref.dtype)

    for n_past in range(k_ref.shape[0] // tq):
        pl.when(qi == n_past)(functools.partial(attend, n_past))


def fox_prompt(q, k, v, cum_col, cum_row, batch, seq, n_heads, tq):
    m, w = q.shape
    nq = seq // tq
    kv_spec = pl.BlockSpec((seq, HEAD_DIM), lambda b, h, qi: (b, h))
    q_spec = pl.BlockSpec((tq, HEAD_DIM), lambda b, h, qi: (b * nq + qi, h))
    return pl.pallas_call(
        _fox_prompt_kernel,
        out_shape=jax.ShapeDtypeStruct((m, w), BF16),
        grid=(batch, n_heads, nq),
        in_specs=[
            q_spec, kv_spec, kv_spec,
            pl.BlockSpec((tq, cum_col.shape[1]), lambda b, h, qi: (b * nq + qi, 0)),
            pl.BlockSpec((None, n_heads, seq), lambda b, h, qi: (b, 0, 0)),
        ],
        out_specs=q_spec,
        compiler_params=_params("parallel", "parallel", "parallel"),
        name="fox_prompt",
    )(q, k, v, cum_col, cum_row)


def _own_head(shape, n_heads):
    row = lax.broadcasted_iota(jnp.int32, shape, 0)
    col = lax.broadcasted_iota(jnp.int32, shape, 1)
    return (col & (n_heads - 1)) == row, col


def _page_rows(ref):
    page, n_heads, hd = ref.shape
    return ref[...].reshape(page * n_heads, hd).astype(BF16)


def _page_spec(cache, index_map):
    page, n_heads, hd = cache.shape[2:]
    return pl.BlockSpec((None, None, page, n_heads, hd), index_map)


def _moba_gate_kernel(pt_ref, q_ref, *refs, pages, n_blocks, n_heads):
    del pt_ref
    k_refs = refs[:pages]
    o_ref = refs[pages]
    means_ref = refs[pages + 1]
    j = pl.program_id(1)
    per_block = MOBA_BLOCK // k_refs[0].shape[0]
    blocks_per_step = pages // per_block
    for r in range(blocks_per_step):
        tot = jnp.sum(k_refs[r * per_block][...], axis=0)
        for t in range(1, per_block):
            tot = tot + jnp.sum(k_refs[r * per_block + t][...], axis=0)
        means_ref[j * blocks_per_step + r] = tot * (1.0 / MOBA_BLOCK)

    @pl.when(j == pl.num_programs(1) - 1)
    def _():
        means = means_ref[...].reshape(n_blocks * n_heads, HEAD_DIM).astype(BF16)
        gate = lax.dot_general(q_ref[...], means, NT, preferred_element_type=F32)
        own, col = _own_head(gate.shape, n_heads)
        gate = jnp.where(own, gate, -jnp.inf)
        blk = lax.shift_right_logical(col, int(math.log2(n_heads))).astype(F32)
        lane = lax.broadcasted_iota(jnp.int32, o_ref.shape, 1)
        out = jnp.full(o_ref.shape, -1.0, F32)
        for r in range(MOBA_TOPK):
            mx = jnp.max(gate, axis=1, keepdims=True)
            ix = jnp.min(jnp.where(gate == mx, blk, float(n_blocks)), axis=1, keepdims=True)
            keep = jnp.where(jnp.abs(mx) < jnp.inf, ix, -1.0)
            out = jnp.where(lane == r, keep, out)
            gate = jnp.where(blk == ix, -jnp.inf, gate)
        o_ref[...] = out.astype(jnp.int32)


def moba_decode_select(page_table, q_heads, cache_k, layer):
    bsz, n_pages = page_table.shape
    page, n_heads, hd = cache_k.shape[2:]
    pages = DECODE_PAGES_PER_STEP
    n_blocks = n_pages * page // MOBA_BLOCK

    def k_map(r):
        return lambda b, j, pt: (pt[b, j * pages + r], layer, 0, 0, 0)

    grid_spec = pltpu.PrefetchScalarGridSpec(
        num_scalar_prefetch=1,
        grid=(bsz, n_pages // pages),
        in_specs=[pl.BlockSpec((None, n_heads, hd), lambda b, j, pt: (b, 0, 0))]
        + [_page_spec(cache_k, k_map(r)) for r in range(pages)],
        out_specs=pl.BlockSpec((None, n_heads, 128), lambda b, j, pt: (b, 0, 0)),
        scratch_shapes=[pltpu.VMEM((n_blocks, n_heads, hd), F32)],
    )
    return pl.pallas_call(
        functools.partial(_moba_gate_kernel, pages=pages, n_blocks=n_blocks, n_heads=n_heads),
        out_shape=jax.ShapeDtypeStruct((bsz, n_heads, 128), jnp.int32),
        grid_spec=grid_spec,
        compiler_params=_params("parallel", "arbitrary"),
        name="moba_decode_select",
    )(page_table, q_heads, *([cache_k] * pages))


def _moba_decode_attn_kernel(pt_ref, sel_ref, q_ref, kn_ref, vn_ref, ck_ref, cv_ref, o_ref, kbuf, vbuf, sem,
                             *, n_sel, per_block, n_heads, n_pages, layer):
    b = pl.program_id(0)
    n = n_sel * per_block
    page = kbuf.shape[3]
    slot = lax.rem(b, 2)

    def copies(req, buf_slot):
        out = []
        for h in range(n_heads):
            for r in range(n_sel):
                blk = jnp.maximum(sel_ref[(req * n_heads + h) * n_sel + r], 0)
                for t in range(per_block):
                    pg = pt_ref[req * n_pages + blk * per_block + t]
                    i = r * per_block + t
                    out.append(pltpu.make_async_copy(ck_ref.at[pg, layer, :, h, :], kbuf.at[buf_slot, h, i],
                                                     sem.at[buf_slot]))
                    out.append(pltpu.make_async_copy(cv_ref.at[pg, layer, :, h, :], vbuf.at[buf_slot, h, i],
                                                     sem.at[buf_slot]))
        return out

    @pl.when(b == 0)
    def _():
        for cp in copies(0, 0):
            cp.start()

    @pl.when(b + 1 < pl.num_programs(0))
    def _():
        for cp in copies(b + 1, 1 - slot):
            cp.start()

    for cp in copies(b, slot):
        cp.wait()

    q = q_ref[...].astype(F32)
    kn = kn_ref[...].astype(F32)
    vn = vn_ref[...].astype(F32)
    outs = []
    for h in range(n_heads):
        qh = q[h:h + 1, :]
        q8 = jnp.broadcast_to(qh, (8, HEAD_DIM)).astype(BF16)
        s_self = jnp.sum(qh * kn[h:h + 1, :], axis=1, keepdims=True) * ATTN_SCALE
        keys = kbuf[slot, h].reshape(n * page, HEAD_DIM).astype(BF16)
        s = lax.dot_general(q8, keys, NT, preferred_element_type=F32) * ATTN_SCALE
        pieces = []
        for r in range(n_sel):
            valid = sel_ref[(b * n_heads + h) * n_sel + r] >= 0
            pieces.append(jnp.where(valid, s[:, r * MOBA_BLOCK:(r + 1) * MOBA_BLOCK], NEG))
        s = jnp.concatenate(pieces, axis=1)
        m = jnp.maximum(jnp.max(s, axis=1, keepdims=True), s_self)
        p = jnp.exp(s - m)
        p_self = jnp.exp(s_self - m)
        l = jnp.sum(p, axis=1, keepdims=True) + p_self
        vals = vbuf[slot, h].reshape(n * page, HEAD_DIM).astype(BF16)
        acc = jnp.dot(p.astype(BF16), vals, preferred_element_type=F32)
        acc = acc + p_self.astype(BF16).astype(F32) * vn[h:h + 1, :]
        outs.append((acc / l)[0:1, :])
    o_ref[...] = jnp.concatenate(outs, axis=1).astype(o_ref.dtype)


def moba_decode_attend(page_table, sel, q_heads, k_new, v_new, cache_k, cache_v, layer):
    bsz, n_pages = page_table.shape
    page, n_heads, hd = cache_k.shape[2:]
    per_block = MOBA_BLOCK // page
    n_sel = sel.shape[-1]
    n = n_sel * per_block
    heads = pl.BlockSpec((None, n_heads, hd), lambda b, pt, sl: (b, 0, 0))
    grid_spec = pltpu.PrefetchScalarGridSpec(
        num_scalar_prefetch=2,
        grid=(bsz,),
        in_specs=[heads, heads, heads, pl.BlockSpec(memory_space=pl.ANY), pl.BlockSpec(memory_space=pl.ANY)],
        out_specs=pl.BlockSpec((None, 1, n_heads * hd), lambda b, pt, sl: (b, 0, 0)),
        scratch_shapes=[pltpu.VMEM((2, n_heads, n, page, hd), cache_k.dtype),
                        pltpu.VMEM((2, n_heads, n, page, hd), cache_v.dtype),
                        pltpu.SemaphoreType.DMA((2,))],
    )
    out = pl.pallas_call(
        functools.partial(_moba_decode_attn_kernel, n_sel=n_sel, per_block=per_block, n_heads=n_heads,
                          n_pages=n_pages, layer=layer),
        out_shape=jax.ShapeDtypeStruct((bsz, 1, n_heads * hd), BF16),
        grid_spec=grid_spec,
        compiler_params=_params("arbitrary"),
        name="moba_decode_attend",
    )(page_table.reshape(-1), sel.reshape(-1), q_heads, k_new, v_new, cache_k, cache_v)
    return out.reshape(bsz, n_heads * hd)


def _fox_decode_kernel(pt_ref, q_ref, kn_ref, vn_ref, lfn_ref, *refs, pages, n_heads):
    del pt_ref
    k_refs = refs[:pages]
    v_refs = refs[pages:2 * pages]
    lf_refs = refs[2 * pages:3 * pages]
    o_ref = refs[3 * pages]
    m_ref, l_ref, carry_ref, acc_ref = refs[3 * pages + 1:]
    j = pl.program_id(1)
    q = q_ref[...]
    width = carry_ref.shape[1]
    own, col = _own_head((n_heads, width), n_heads)
    col1 = col[0:1, :]

    @pl.when(j == 0)
    def _():
        s_self = jnp.sum(q.astype(F32) * kn_ref[...].astype(F32), axis=1, keepdims=True) * ATTN_SCALE
        m_ref[...] = jnp.broadcast_to(s_self, m_ref.shape)
        l_ref[...] = jnp.ones(l_ref.shape, F32)
        acc_ref[...] = vn_ref[...].astype(F32)
        lanes = lfn_ref.shape[1]
        t = jnp.where(lax.broadcasted_iota(jnp.int32, (1, lanes), 1) < n_heads, lfn_ref[...], 0.0)
        sh = n_heads
        while sh < lanes:
            t = t + pltpu.roll(t, sh, 1)
            sh *= 2
        carry_ref[...] = jnp.concatenate([t] * (width // lanes), axis=1)

    carry = carry_ref[...]
    scores = []
    for r in range(pages):
        lf = lf_refs[r][...]
        suf = lf
        tot = lf
        sh = n_heads
        while sh < width:
            suf = suf + jnp.where(col1 + sh < width, pltpu.roll(suf, width - sh, 1), 0.0)
            tot = tot + pltpu.roll(tot, sh, 1)
            sh *= 2
        bias = carry + (suf - lf)
        carry = carry + tot
        s = lax.dot_general(q, _page_rows(k_refs[r]), NT, preferred_element_type=F32) * ATTN_SCALE + bias
        scores.append(jnp.where(own, s, NEG))
    carry_ref[...] = carry

    m_old = m_ref[...]
    mx = jnp.max(scores[0], axis=1, keepdims=True)
    for s in scores[1:]:
        mx = jnp.maximum(mx, jnp.max(s, axis=1, keepdims=True))
    m_new = jnp.maximum(m_old, mx)
    alpha = jnp.exp(m_old - m_new)
    l_new = alpha * l_ref[...]
    acc = alpha[:, 0:1] * acc_ref[...]
    for r in range(pages):
        p = jnp.exp(scores[r] - m_new[:, 0:1])
        l_new = l_new + jnp.sum(p, axis=1, keepdims=True)
        acc = acc + jnp.dot(p.astype(BF16), _page_rows(v_refs[r]), preferred_element_type=F32)
    m_ref[...] = m_new
    l_ref[...] = l_new
    acc_ref[...] = acc

    @pl.when(j == pl.num_programs(1) - 1)
    def _():
        o_ref[...] = (acc_ref[...] / l_ref[:, 0:1]).astype(o_ref.dtype)


def fox_decode(page_table, q_heads, k_new, v_new, lf_new, cache_k, cache_v, cache_lf, layer):
    bsz, n_pages = page_table.shape
    page, n_heads, hd = cache_k.shape[2:]
    pages = DECODE_PAGES_PER_STEP
    width = page * n_heads

    def page_map(r):
        return lambda b, j, pt: (pt[b, n_pages - 1 - (j * pages + r)], layer, 0, 0, 0)

    def lf_map(r):
        return lambda b, j, pt: (pt[b, n_pages - 1 - (j * pages + r)], layer, 0, 0)

    heads = pl.BlockSpec((None, n_heads, hd), lambda b, j, pt: (b, 0, 0))
    grid_spec = pltpu.PrefetchScalarGridSpec(
        num_scalar_prefetch=1,
        grid=(bsz, n_pages // pages),
        in_specs=[heads, heads, heads, pl.BlockSpec((None, 1, lf_new.shape[-1]), lambda b, j, pt: (b, 0, 0))]
        + [_page_spec(cache_k, page_map(r)) for r in range(pages)]
        + [_page_spec(cache_v, page_map(r)) for r in range(pages)]
        + [pl.BlockSpec((None, None, 1, width), lf_map(r)) for r in range(pages)],
        out_specs=heads,
        scratch_shapes=[pltpu.VMEM((n_heads, 128), F32), pltpu.VMEM((n_heads, 128), F32),
                        pltpu.VMEM((1, width), F32), pltpu.VMEM((n_heads, hd), F32)],
    )
    out = pl.pallas_call(
        functools.partial(_fox_decode_kernel, pages=pages, n_heads=n_heads),
        out_shape=jax.ShapeDtypeStruct((bsz, n_heads, hd), BF16),
        grid_spec=grid_spec,
        compiler_params=_params("parallel", "arbitrary"),
        name="fox_decode",
    )(page_table, q_heads, k_new, v_new, lf_new,
      *([cache_k] * pages), *([cache_v] * pages), *([cache_lf] * pages))
    return out.reshape(bsz, n_heads * hd)


def _merge_kernel(oa_ref, of_ref, ga_ref, gb_ref, wa_ref, wb_ref, wo_ref, x_ref, g_ref, x1_ref, xn_ref):
    ya = jnp.dot(oa_ref[...], wa_ref[...], preferred_element_type=F32)
    yf = jnp.dot(of_ref[...], wb_ref[...], preferred_element_type=F32)
    mix = ga_ref[...] * ya + gb_ref[...] * yf
    x1 = x_ref[...] + jnp.dot(mix.astype(BF16), wo_ref[...], preferred_element_type=F32)
    x1_ref[...] = x1
    xn_ref[...] = _rms_scale(x1, g_ref[...]).astype(BF16)


def merge_out(oa, of, gates, wa, wb, wo, x, g_next):
    m, d = x.shape
    wdt = oa.shape[1]
    row = lambda width: pl.BlockSpec((ROW_TILE, width), lambda i: (i, 0))
    return pl.pallas_call(
        _merge_kernel,
        out_shape=(jax.ShapeDtypeStruct((m, d), F32), jax.ShapeDtypeStruct((m, d), BF16)),
        grid=(m // ROW_TILE,),
        in_specs=[row(wdt), row(wdt), row(d), pl.BlockSpec((ROW_TILE, d), lambda i: (i, 1)),
                  _resident(wa.shape, lambda i: (0, 0)), _resident(wb.shape, lambda i: (0, 0)),
                  _resident(wo.shape, lambda i: (0, 0)), row(d), _resident((1, d), lambda i: (0, 0))],
        out_specs=(row(d), row(d)),
        compiler_params=_params("parallel"),
        name="merge_out",
    )(oa, of, gates, gates, wa, wb, wo, x, g_next.reshape(1, d))


def _matmul_bf16_kernel(a_ref, w_ref, o_ref):
    o_ref[...] = jnp.dot(a_ref[...], w_ref[...], preferred_element_type=F32).astype(o_ref.dtype)


def matmul_bf16(a, w, tn):
    m, d = a.shape
    n = w.shape[1]
    return pl.pallas_call(
        _matmul_bf16_kernel,
        out_shape=jax.ShapeDtypeStruct((m, n), BF16),
        grid=(n // tn, m // ROW_TILE),
        in_specs=[pl.BlockSpec((ROW_TILE, d), lambda j, i: (i, 0)), pl.BlockSpec((d, tn), lambda j, i: (0, j))],
        out_specs=pl.BlockSpec((ROW_TILE, tn), lambda j, i: (i, j)),
        compiler_params=_params("parallel", "parallel"),
        name="peer_query",
    )(a, w)


def _top_values(x, k):
    vals = []
    for _ in range(k):
        mx = jnp.max(x, axis=0, keepdims=True)
        vals.append(mx)
        x = jnp.where(x == mx, NEG, x)
    return vals


def _peer_route_kernel(q_ref, k1_ref, k2_ref, s1_ref, s2_ref, e1_ref, e2_ref, thr_ref, *, n_heads):
    dk = k1_ref.shape[1]
    k1 = k1_ref[...]
    k2 = k2_ref[...]
    for h in range(n_heads):
        q1 = q_ref[:, (2 * h) * dk:(2 * h + 1) * dk]
        q2 = q_ref[:, (2 * h + 1) * dk:(2 * h + 2) * dk]
        s1 = lax.dot_general(k1, q1, NT, preferred_element_type=F32)
        s2 = lax.dot_general(k2, q2, NT, preferred_element_type=F32)
        t1 = _top_values(s1, PEER_TOPK)
        t2 = _top_values(s2, PEER_TOPK)
        half = PEER_TOPK // 2
        t2_all = jnp.concatenate(t2, axis=0)
        t2_low = jnp.concatenate(t2[:half], axis=0)
        b_rank = lax.broadcasted_iota(jnp.int32, t2_low.shape, 0)
        cand = [t1[0] + t2_all]
        for a in range(1, half):
            cand.append(jnp.where(b_rank < PEER_TOPK // (a + 1), t1[a] + t2_low, NEG))
        cand.append(jnp.concatenate(t1[half:], axis=0) + t2[0])
        top = _top_values(jnp.concatenate(cand, axis=0), PEER_TOPK)
        z = jnp.zeros_like(top[0])
        for val in top:
            z = z + jnp.exp(val - top[0])
        s1_ref[h] = s1
        s2_ref[h] = s2
        e1_ref[h] = jnp.exp(s1 - t1[0])
        e2_ref[h] = jnp.exp(s2 - t2[0]) / z
        thr_ref[pl.ds(h, 1), :] = top[PEER_TOPK - 1]


def peer_route(qp, k1, k2, n_heads):
    m, w = qp.shape
    n_keys = k1.shape[0]
    big = jax.ShapeDtypeStruct((n_heads, n_keys, m), F32)
    big_spec = pl.BlockSpec((n_heads, n_keys, ROW_TILE), lambda i: (0, 0, i))
    return pl.pallas_call(
        functools.partial(_peer_route_kernel, n_heads=n_heads),
        out_shape=(big, big, big, big, jax.ShapeDtypeStruct((n_heads, m), F32)),
        grid=(m // ROW_TILE,),
        in_specs=[pl.BlockSpec((ROW_TILE, w), lambda i: (i, 0)), pl.BlockSpec(k1.shape, lambda i: (0, 0)),
                  pl.BlockSpec(k2.shape, lambda i: (0, 0))],
        out_specs=(big_spec, big_spec, big_spec, big_spec, pl.BlockSpec((n_heads, ROW_TILE), lambda i: (0, i))),
        compiler_params=_params("parallel"),
        name="peer_route",
    )(qp, k1, k2)


def _peer_dense_kernel(xn_ref, u_ref, v_ref, s1p_ref, s1c_ref, s2_ref, e1p_ref, e1c_ref, e2_ref, thr_ref, x_ref,
                       o_ref, at0_ref, at1_ref, cg0_ref, cg1_ref, *, n_heads, n_keys):
    k = pl.program_id(1)
    th, tm = at0_ref.shape
    groups = th // n_keys
    chunk = 32
    chunks = n_keys // chunk
    lane_tile = 128

    @pl.when(k == 0)
    def _():
        o_ref[...] = x_ref[...]

    def activations(half, at_ref):
        at_ref[...] = lax.dot_general(u_ref[half * th:(half + 1) * th, :], xn_ref[...], NT,
                                      preferred_element_type=F32)

    def coefficients(s1_ref, e1_ref, half, at_ref, cg_ref):
        for lt in range(tm // lane_tile):
            lanes = slice(lt * lane_tile, (lt + 1) * lane_tile)
            for c in range(chunks):
                rows = slice(c * chunk, (c + 1) * chunk)
                accs = [jnp.zeros((chunk, lane_tile), F32) for _ in range(groups)]
                for h in range(n_heads):
                    s2c = s2_ref[h, rows, lanes]
                    e2c = e2_ref[h, rows, lanes]
                    thr = thr_ref[h:h + 1, lanes]
                    for ig in range(groups):
                        r = half * groups + ig
                        pair = s2c + s1_ref[h, r:r + 1, lanes]
                        wgt = e2c * e1_ref[h, r:r + 1, lanes]
                        accs[ig] = jnp.where(pair >= thr, accs[ig] + wgt, accs[ig])
                for ig in range(groups):
                    a_rows = slice(ig * n_keys + c * chunk, ig * n_keys + (c + 1) * chunk)
                    a = at_ref[a_rows, lanes]
                    gelu = 0.5 * a * (1.0 + lax.erf(a * SQRT_HALF))
                    cg_ref[a_rows, lanes] = (accs[ig] * gelu).astype(BF16)

    def accumulate(half, cg_ref):
        o_ref[...] += lax.dot_general(cg_ref[...], v_ref[half * th:(half + 1) * th, :], TN,
                                      preferred_element_type=F32)

    last = pl.num_programs(1) - 1

    @pl.when(k == 0)
    def _():
        activations(0, at0_ref)
        activations(1, at1_ref)
        coefficients(s1c_ref, e1c_ref, 0, at0_ref, cg0_ref)

    @pl.when(jnp.logical_and(k > 0, k < last))
    def _():
        activations(0, at0_ref)
        coefficients(s1p_ref, e1p_ref, 1, at1_ref, cg1_ref)
        accumulate(0, cg0_ref)
        activations(1, at1_ref)
        coefficients(s1c_ref, e1c_ref, 0, at0_ref, cg0_ref)
        accumulate(1, cg1_ref)

    @pl.when(k == last)
    def _():
        coefficients(s1p_ref, e1p_ref, 1, at1_ref, cg1_ref)
        accumulate(0, cg0_ref)
        accumulate(1, cg1_ref)


def peer_dense(xn, u, v, s1, s2, e1, e2, thr, x):
    m, d = xn.shape
    n_exp = u.shape[0]
    n_heads, n_keys = s1.shape[:2]
    te = PEER_EXPERT_TILE
    th = te // 2
    n_tiles = n_exp // te
    per_tile = te // n_keys
    cur = lambda i, k: (0, jnp.minimum(k, n_tiles - 1), i)
    prev = lambda i, k: (0, jnp.maximum(k - 1, 0), i)
    rows_spec = lambda index_map: pl.BlockSpec((n_heads, per_tile, ROW_TILE), index_map)
    big_spec = pl.BlockSpec((n_heads, n_keys, ROW_TILE), lambda i, k: (0, 0, i))
    return pl.pallas_call(
        functools.partial(_peer_dense_kernel, n_heads=n_heads, n_keys=n_keys),
        out_shape=jax.ShapeDtypeStruct((m, d), F32),
        grid=(m // ROW_TILE, n_tiles + 1),
        in_specs=[pl.BlockSpec((ROW_TILE, d), lambda i, k: (i, 0)),
                  pl.BlockSpec((te, d), lambda i, k: (jnp.minimum(k, n_tiles - 1), 0)),
                  pl.BlockSpec((te, d), lambda i, k: (jnp.maximum(k - 1, 0), 0)),
                  rows_spec(prev), rows_spec(cur), big_spec, rows_spec(prev), rows_spec(cur), big_spec,
                  pl.BlockSpec((n_heads, ROW_TILE), lambda i, k: (0, i)),
                  pl.BlockSpec((ROW_TILE, d), lambda i, k: (i, 0))],
        out_specs=pl.BlockSpec((ROW_TILE, d), lambda i, k: (i, 0)),
        scratch_shapes=[pltpu.VMEM((th, ROW_TILE), F32), pltpu.VMEM((th, ROW_TILE), F32),
                        pltpu.VMEM((th, ROW_TILE), BF16), pltpu.VMEM((th, ROW_TILE), BF16)],
        compiler_params=_params("parallel", "arbitrary"),
        name="peer_dense",
    )(xn, u, v, s1, s1, s2, e1, e1, e2, thr, x)


def _ple_kernel(x_ref, p_ref, g_ref, wg_ref, wp_ref, gf_ref, y_ref):
    x = x_ref[...]
    gate = _sigmoid(jnp.dot(_rms_scale(x, g_ref[...]).astype(BF16), wg_ref[...], preferred_element_type=F32))
    emb = jnp.dot(p_ref[...].astype(BF16), wp_ref[...], preferred_element_type=F32)
    y_ref[...] = _rms_scale(x + gate * emb, gf_ref[...])


def ple_final(x, p, g_ple, wg, wp, g_final):
    m, d = x.shape
    row = lambda width: pl.BlockSpec((ROW_TILE, width), lambda i: (i, 0))
    vec = _resident((1, d), lambda i: (0, 0))
    return pl.pallas_call(
        _ple_kernel,
        out_shape=jax.ShapeDtypeStruct((m, d), F32),
        grid=(m // ROW_TILE,),
        in_specs=[row(d), row(p.shape[1]), vec, _resident(wg.shape, lambda i: (0, 0)),
                  _resident(wp.shape, lambda i: (0, 0)), vec],
        out_specs=row(d),
        compiler_params=_params("parallel"),
        name="ple_final",
    )(x, p, g_ple.reshape(1, d), wg, wp, g_final.reshape(1, d))


def _rope_tables(pos):
    half = HEAD_DIM // 2
    inv_freq = ROPE_THETA ** (-jnp.arange(half, dtype=F32) / half)
    ang = pos.astype(F32)[:, None] * inv_freq[None, :]
    cos, sin = jnp.cos(ang), jnp.sin(ang)
    return jnp.concatenate([cos, cos], axis=1), jnp.concatenate([-sin, sin], axis=1)


def _pad_rows(a, rows):
    return jnp.pad(a, ((0, rows - a.shape[0]),) + ((0, 0),) * (a.ndim - 1))


def _token_stage_in(x, pos, g_attn, w_qkv, w_fl, b_fl, w_gates, seg):
    h = rms_cast(x, g_attn)
    cos, sin = _rope_tables(pos)
    zf, zb = [], []
    for j in range(w_qkv.shape[1] // seg):
        is_cache_out = j % 3 != 0
        outs = proj_heads(h, w_qkv[:, j * seg:(j + 1) * seg], cos, sin, rope=j < 2, keep_f32=is_cache_out)
        zb.append(outs[0])
        zf.append(outs[1] if is_cache_out else None)
    gates = proj_sigmoid(h, w_gates, tn=1024)
    lf = proj_logf(h, w_fl, b_fl)
    return zf, zb, gates, lf


def _token_stage_out(x, oa, of, gates, p, wts):
    x1, xn = merge_out(oa, of, gates, wts["wa"], wts["wb"], wts["wo"], x, wts["g_ffn"])
    qp = matmul_bf16(xn, wts["wq"], tn=1024)
    s1, s2, e1, e2, thr = peer_route(qp, wts["k1"], wts["k2"], wts["peer_heads"])
    x2 = peer_dense(xn, wts["u"], wts["v"], s1, s2, e1, e2, thr, x1)
    return ple_final(x2, p, wts["g_ple"], wts["wg"], wts["wp"], wts["g_final"])


def kernel(x_prompt, x_sample, cache_moba_k, cache_moba_v, cache_fox_k, cache_fox_v, cache_fox_logf, page_table, p_prompt, p_sample, g_attn, w_in, b_forget, w_branch_a, w_branch_b, w_out, g_ffn, w_peer_q, peer_subkey_1, peer_subkey_2, peer_u, peer_v, g_ple, w_ple_gate, w_ple_proj, g_final):
    batch, seq, d = x_prompt.shape
    dec_batch, dec_seq, _ = x_sample.shape
    n_pool, depth, page, n_moba, hd = cache_moba_k.shape
    n_fox = cache_fox_k.shape[3]
    n_pages = page_table.shape[1]
    past_len = n_pages * page
    wa_w, wb_w = n_moba * hd, n_fox * hd
    assert depth == 1 and dec_seq == 1 and hd == HEAD_DIM and wa_w == wb_w
    assert n_moba & (n_moba - 1) == 0 and n_fox & (n_fox - 1) == 0
    assert seq % MOBA_BLOCK == 0 and past_len % MOBA_BLOCK == 0 and MOBA_BLOCK % page == 0
    layer = 0
    n_qkv = 3 * wa_w + 3 * wb_w
    peer_heads = w_peer_q.shape[2] // (2 * peer_subkey_1.shape[2])
    lanes = 128

    w_l = w_in[layer]
    w_qkv = w_l[:, :n_qkv].astype(BF16)
    w_fl = jnp.pad(w_l[:, n_qkv:n_qkv + n_fox], ((0, 0), (0, lanes - n_fox))).astype(BF16)
    b_fl = jnp.pad(b_forget[layer], (0, lanes - n_fox)).reshape(1, lanes)
    w_gates = w_l[:, n_qkv + n_fox:].astype(BF16)
    wts = dict(
        wa=w_branch_a[layer].astype(BF16), wb=w_branch_b[layer].astype(BF16), wo=w_out[layer].astype(BF16),
        g_ffn=g_ffn[layer], wq=w_peer_q[layer].astype(BF16),
        k1=peer_subkey_1[layer].astype(BF16), k2=peer_subkey_2[layer].astype(BF16), peer_heads=peer_heads,
        u=peer_u[layer].astype(BF16), v=peer_v[layer].astype(BF16),
        g_ple=g_ple[layer], wg=w_ple_gate[layer].astype(BF16), wp=w_ple_proj[layer].astype(BF16), g_final=g_final,
    )

    m_p = batch * seq
    xp = x_prompt.reshape(m_p, d)
    pos_p = jnp.tile(jnp.arange(seq, dtype=jnp.int32), batch)
    zf_p, zb_p, gates_p, lf_p = _token_stage_in(xp, pos_p, g_attn[layer].reshape(1, d), w_qkv, w_fl, b_fl, w_gates, wa_w)
    cum_col, cum_row = cumsum_logf(lf_p, batch, seq, n_fox)
    oa_p = moba_prompt(zb_p[0], zb_p[1], zb_p[2], zf_p[1], batch, seq, n_moba)
    of_p = fox_prompt(zb_p[3], zb_p[4], zb_p[5], cum_col, cum_row, batch, seq, n_fox, tq=MOBA_BLOCK)
    y_p = _token_stage_out(xp, oa_p, of_p, gates_p, p_prompt[layer].reshape(m_p, -1), wts)

    m_s = ROW_TILE * (-(-dec_batch // ROW_TILE))
    xs = _pad_rows(x_sample.reshape(dec_batch, d), m_s)
    pos_s = jnp.full((m_s,), past_len, jnp.int32)
    zf_s, zb_s, gates_s, lf_s = _token_stage_in(xs, pos_s, g_attn[layer].reshape(1, d), w_qkv, w_fl, b_fl, w_gates, wa_w)
    per_head = lambda seg, n: zb_s[seg][:dec_batch].reshape(dec_batch, n, hd)
    qa_s = per_head(0, n_moba)
    sel = moba_decode_select(page_table, qa_s, cache_moba_k, layer)[:, :, :MOBA_TOPK]
    oa_s = moba_decode_attend(page_table, sel, qa_s, per_head(1, n_moba), per_head(2, n_moba),
                              cache_moba_k, cache_moba_v, layer)
    lf_pages = cache_fox_logf.reshape(n_pool, depth, 1, page * n_fox)
    of_s = fox_decode(page_table, per_head(3, n_fox), per_head(4, n_fox), per_head(5, n_fox),
                      lf_s[:dec_batch].reshape(dec_batch, 1, lanes), cache_fox_k, cache_fox_v, lf_pages, layer)
    y_s = _token_stage_out(xs, _pad_rows(oa_s, m_s), _pad_rows(of_s, m_s), gates_s,
                           _pad_rows(p_sample[layer].reshape(dec_batch, -1), m_s), wts)

    def kv_p(a, n):
        return a.reshape(batch, 1, seq, n, hd)

    def kv_s(a, n):
        return a[:dec_batch].reshape(dec_batch, 1, 1, n, hd)

    return (y_p.reshape(batch, seq, d), y_s[:dec_batch].reshape(dec_batch, 1, d),
            kv_p(zf_p[1], n_moba), kv_p(zf_p[2], n_moba), kv_p(zf_p[4], n_fox), kv_p(zf_p[5], n_fox),
            lf_p[:, :n_fox].reshape(batch, 1, seq, n_fox),
            kv_s(zf_s[1], n_moba), kv_s(zf_s[2], n_moba), kv_s(zf_s[4], n_fox), kv_s(zf_s[5], n_fox),
            lf_s[:dec_batch, :n_fox].reshape(dec_batch, 1, 1, n_fox))
```
